```python
import jax
import jax.numpy as jnp
from jax import lax
import numpy as np

D_MODEL = 1024
BATCH = 2
SEQ = 8192
DEPTH = 4
DEC_BATCH = 128
DEC_SEQ = 8
PAST_LEN = 8192
PAGE_SIZE = 128

N_BRANCH = 3
BRANCH_WIDTH = 512
MLA_HEADS = 8
MLA_NOPE = 64
MLA_ROPE = 32
MLA_V = 64
MLA_Q_LORA = 256
MLA_KV_LORA = 128
MLA_SCALE = (MLA_NOPE + MLA_ROPE) ** -0.5
ROPE_THETA = 10000.0
GMLP_CHUNK = 128
GMLP_GROUPS = 4
GMLP_GROUP_DIM = 128
GMLP_WIDTH = GMLP_GROUPS * GMLP_GROUP_DIM
SB_HEADS = 4
SB_KV_HEADS = 2
SB_GROUP = SB_HEADS // SB_KV_HEADS
SB_HEAD_DIM = 128
SB_SCALE = SB_HEAD_DIM ** -0.5
FFN_DIM = 2816
CONV_W = 3
Q_BLOCK = 128
EPS = 1e-6
NEG_INF = -1e30
IN_WIDTHS = (MLA_Q_LORA, MLA_KV_LORA, MLA_ROPE, GMLP_WIDTH, GMLP_WIDTH, SB_HEADS * SB_HEAD_DIM, SB_KV_HEADS * SB_HEAD_DIM, SB_KV_HEADS * SB_HEAD_DIM, N_BRANCH * D_MODEL)
D_IN = sum(IN_WIDTHS)

kernel_name = 'hybrid_mla_gmlp_stickbreak_convffn_step'


def rmsnorm(x, g):
    xf = x.astype(jnp.float32)
    y = xf * lax.rsqrt(jnp.mean(xf * xf, axis=-1, keepdims=True) + EPS)
    return (y * g.astype(jnp.float32)).astype(x.dtype)


def rope(x, pos):
    half = x.shape[-1] // 2
    inv_freq = ROPE_THETA ** (-jnp.arange(half, dtype=jnp.float32) / half)
    ang = pos.astype(jnp.float32)[:, None] * inv_freq
    ang = ang.reshape((ang.shape[0],) + (1,) * (x.ndim - 3) + (half,))
    cos, sin = jnp.cos(ang), jnp.sin(ang)
    xf = x.astype(jnp.float32)
    x1, x2 = xf[..., :half], xf[..., half:]
    return jnp.concatenate([x1 * cos - x2 * sin, x2 * cos + x1 * sin], axis=-1).astype(x.dtype)


def weighted_sum(w, values, eq):
    out, start = None, 0
    for v in values:
        n = v.shape[1]
        part = jnp.einsum(eq, lax.slice_in_dim(w, start, start + n, axis=w.ndim - 1), v)
        out = part if out is None else out + part
        start += n
    return out


def mla_attend(q_lat, q_rope, kv_segs, q_pos, k_pos):
    s = jnp.concatenate([
        jnp.einsum('bqhc,bsc->bhqs', q_lat, c, preferred_element_type=jnp.float32)
        + jnp.einsum('bqhr,bsr->bhqs', q_rope, r, preferred_element_type=jnp.float32)
        for c, r in kv_segs], axis=-1) * MLA_SCALE
    s = jnp.where(k_pos[None, :] <= q_pos[:, None], s, NEG_INF)
    p = jax.nn.softmax(s, axis=-1).astype(q_lat.dtype)
    return weighted_sum(p, [c for c, _ in kv_segs], 'bhqs,bsc->bqhc')


def sb_attend(q, kv_segs, q_pos, k_pos):
    z = jnp.concatenate([
        jnp.einsum('bqngd,bsnd->bngqs', q, k, preferred_element_type=jnp.float32)
        for k, _ in kv_segs], axis=-1) * SB_SCALE
    visible = k_pos[None, :] < q_pos[:, None]
    log_keep = jnp.where(visible, jax.nn.log_sigmoid(-z), 0.0)
    log_between = lax.cumsum(log_keep, axis=4, reverse=True) - log_keep
    a = jnp.where(visible, jnp.exp(jax.nn.log_sigmoid(z) + log_between), 0.0).astype(q.dtype)
    return weighted_sum(a, [v for _, v in kv_segs], 'bngqs,bsnd->bqngd')


def sweep_query_blocks(fn, q_args, q_pos):
    t = q_pos.shape[0]
    if t <= Q_BLOCK or t % Q_BLOCK:
        return fn(q_args, q_pos)
    nb = t // Q_BLOCK
    to_blocks = lambda a: jnp.moveaxis(a.reshape((a.shape[0], nb, Q_BLOCK) + a.shape[2:]), 1, 0)
    out = lax.map(lambda xs: fn(xs[0], xs[1]), (tuple(to_blocks(a) for a in q_args), q_pos.reshape(nb, Q_BLOCK)))
    out = jnp.moveaxis(out, 0, 1)
    return out.reshape((out.shape[0], t) + out.shape[3:])


def chunk_spatial_mix(v, w_s, b_s):
    b, t, width = v.shape
    n = -(-t // GMLP_CHUNK)
    vp = jnp.pad(v, ((0, 0), (0, n * GMLP_CHUNK - t), (0, 0)))
    vp = vp.reshape(b, n, GMLP_CHUNK, GMLP_GROUPS, GMLP_GROUP_DIM)
    ws = w_s * jnp.tril(jnp.ones((GMLP_CHUNK, GMLP_CHUNK), w_s.dtype))
    out = jnp.einsum('gts,bnsgc->bntgc', ws, vp) + b_s.T[:, :, None]
    return out.reshape(b, n * GMLP_CHUNK, width)[:, :t]


def gather_pages(pool, layer, page_table):
    rows = pool[layer, page_table]
    return rows.reshape((rows.shape[0], rows.shape[1] * rows.shape[2]) + rows.shape[3:])


def trunk_layer(x, pos, past, w):
    b, t, _ = x.shape
    h = rmsnorm(x, w['norm_mix'])
    split_at = np.cumsum(IN_WIDTHS)[:-1].tolist()
    c_q, c_kv, k_rope, g_u, g_v, s_q, s_k, s_v, gate_logits = jnp.split(h @ w['w_in'], split_at, axis=-1)

    q = (rmsnorm(c_q, w['mla_q_norm']) @ w['mla_w_uq']).reshape(b, t, MLA_HEADS, MLA_NOPE + MLA_ROPE)
    q_lat = jnp.einsum('bthn,chn->bthc', q[..., :MLA_NOPE], w['mla_w_uk'])
    q_rope = rope(q[..., MLA_NOPE:], pos)
    c_kv = rmsnorm(c_kv, w['mla_kv_norm'])
    k_rope = rope(k_rope, pos)
    s_k = s_k.reshape(b, t, SB_KV_HEADS, SB_HEAD_DIM)
    s_v = s_v.reshape(b, t, SB_KV_HEADS, SB_HEAD_DIM)
    if past is None:
        mla_segs = ((c_kv, k_rope),)
        sb_segs = ((s_k, s_v),)
        k_pos = pos
        conv_prefix = jnp.zeros((b, CONV_W - 1, 2 * FFN_DIM), x.dtype)
    else:
        past_len = past['mla_latent'].shape[1]
        mla_segs = ((past['mla_latent'], past['mla_krope']), (c_kv, k_rope))
        sb_segs = ((past['sb_k'], past['sb_v']), (s_k, s_v))
        k_pos = jnp.concatenate([jnp.arange(past_len, dtype=pos.dtype), pos])
        conv_prefix = past['ffn_conv']
    o_lat = sweep_query_blocks(lambda qs, qp: mla_attend(qs[0], qs[1], mla_segs, qp, k_pos), (q_lat, q_rope), pos)
    o_mla = jnp.einsum('bthc,chv->bthv', o_lat, w['mla_w_uv']).reshape(b, t, MLA_HEADS * MLA_V)

    g_v = rmsnorm(jax.nn.gelu(g_v), w['gmlp_v_norm'])
    o_gmlp = jax.nn.gelu(g_u) * chunk_spatial_mix(g_v, w['gmlp_w_s'], w['gmlp_b_s'])

    s_q = s_q.reshape(b, t, SB_KV_HEADS, SB_GROUP, SB_HEAD_DIM)
    o_sb = sweep_query_blocks(lambda qs, qp: sb_attend(qs[0], sb_segs, qp, k_pos), (s_q,), pos)
    o_sb = o_sb.reshape(b, t, SB_HEADS * SB_HEAD_DIM)

    branches = jnp.stack([o_mla, o_gmlp, o_sb], axis=2)
    proj = jnp.einsum('btnw,nwd->btnd', branches, w['w_branch'])
    gates = jax.nn.sigmoid(gate_logits.reshape(b, t, N_BRANCH, D_MODEL))
    x = x + jnp.sum(gates * proj, axis=2) @ w['w_out']

    up = rmsnorm(x, w['norm_ffn']) @ w['ffn_w_up']
    up_ext = jnp.concatenate([conv_prefix, up], axis=1)
    conv = w['ffn_conv_b'] + w['ffn_conv_w'][CONV_W - 1] * up
    for i in range(CONV_W - 1):
        conv = conv + w['ffn_conv_w'][i] * up_ext[:, i:i + t]
    gate, val = jnp.split(conv, 2, axis=-1)
    x = x + (jax.nn.silu(gate) * val) @ w['ffn_w_down']
    return (x, c_kv, k_rope, s_k, s_v, g_v, up_ext[:, t:])


def setup_inputs(seed: int = 0) -> dict:
    key = jax.random.key(seed)
    keys = iter(jax.random.split(key, 32))
    n_pages = PAST_LEN // PAGE_SIZE
    n_pool = (DEC_BATCH * n_pages * 5) // 4
    nrm = lambda shape, scale=1.0: jax.random.normal(next(keys), shape, jnp.float32) * scale
    gain = lambda shape: 1.0 + 0.01 * nrm(shape)
    page_table = jax.random.permutation(next(keys), n_pool)[:DEC_BATCH * n_pages]
    page_table = page_table.reshape(DEC_BATCH, n_pages).astype(jnp.int32)
    return {
        'x_prompt': nrm((BATCH, SEQ, D_MODEL)),
        'x_sample': nrm((DEC_BATCH, DEC_SEQ, D_MODEL)),
        'cache_mla_latent': nrm((DEPTH, n_pool, PAGE_SIZE, MLA_KV_LORA)),
        'cache_mla_krope': nrm((DEPTH, n_pool, PAGE_SIZE, MLA_ROPE)),
        'cache_sb_k': nrm((DEPTH, n_pool, PAGE_SIZE, SB_KV_HEADS, SB_HEAD_DIM)),
        'cache_sb_v': nrm((DEPTH, n_pool, PAGE_SIZE, SB_KV_HEADS, SB_HEAD_DIM)),
        'state_ffn_conv': nrm((DEPTH, DEC_BATCH, CONV_W - 1, 2 * FFN_DIM)),
        'page_table': page_table,
        'norm_mix': gain((DEPTH, D_MODEL)),
        'w_in': nrm((DEPTH, D_MODEL, D_IN), D_MODEL ** -0.5),
        'mla_q_norm': gain((DEPTH, MLA_Q_LORA)),
        'mla_w_uq': nrm((DEPTH, MLA_Q_LORA, MLA_HEADS * (MLA_NOPE + MLA_ROPE)), MLA_Q_LORA ** -0.5),
        'mla_kv_norm': gain((DEPTH, MLA_KV_LORA)),
        'mla_w_uk': nrm((DEPTH, MLA_KV_LORA, MLA_HEADS, MLA_NOPE), MLA_KV_LORA ** -0.5),
        'mla_w_uv': nrm((DEPTH, MLA_KV_LORA, MLA_HEADS, MLA_V), MLA_KV_LORA ** -0.5),
        'gmlp_v_norm': gain((DEPTH, GMLP_WIDTH)),
        'gmlp_w_s': nrm((DEPTH, GMLP_GROUPS, GMLP_CHUNK, GMLP_CHUNK), GMLP_CHUNK ** -0.5),
        'gmlp_b_s': gain((DEPTH, GMLP_GROUPS, GMLP_CHUNK)),
        'w_branch': nrm((DEPTH, N_BRANCH, BRANCH_WIDTH, D_MODEL), BRANCH_WIDTH ** -0.5),
        'w_out': nrm((DEPTH, D_MODEL, D_MODEL), D_MODEL ** -0.5),
        'norm_ffn': gain((DEPTH, D_MODEL)),
        'ffn_w_up': nrm((DEPTH, D_MODEL, 2 * FFN_DIM), D_MODEL ** -0.5),
        'ffn_conv_w': nrm((DEPTH, CONV_W, 2 * FFN_DIM), CONV_W ** -0.5),
        'ffn_conv_b': nrm((DEPTH, 2 * FFN_DIM), 0.01),
        'ffn_w_down': nrm((DEPTH, FFN_DIM, D_MODEL), FFN_DIM ** -0.5),
        'norm_final': gain((D_MODEL,)),
    }


def reference(x_prompt, x_sample, cache_mla_latent, cache_mla_krope, cache_sb_k, cache_sb_v, state_ffn_conv, page_table,
              norm_mix, w_in, mla_q_norm, mla_w_uq, mla_kv_norm, mla_w_uk, mla_w_uv, gmlp_v_norm, gmlp_w_s, gmlp_b_s,
              w_branch, w_out, norm_ffn, ffn_w_up, ffn_conv_w, ffn_conv_b, ffn_w_down, norm_final):
    pos_p = jnp.arange(x_prompt.shape[1], dtype=jnp.int32)
    past_len = page_table.shape[1] * PAGE_SIZE
    pos_s = past_len + jnp.arange(x_sample.shape[1], dtype=jnp.int32)
    xp, xs = x_prompt, x_sample
    p_lat, p_rope, p_k, p_v, p_conv = [], [], [], [], []
    s_lat, s_rope, s_k, s_v, s_gv, s_conv = [], [], [], [], [], []
    for l in range(DEPTH):
        w = dict(norm_mix=norm_mix[l], w_in=w_in[l], mla_q_norm=mla_q_norm[l], mla_w_uq=mla_w_uq[l],
                 mla_kv_norm=mla_kv_norm[l], mla_w_uk=mla_w_uk[l], mla_w_uv=mla_w_uv[l],
                 gmlp_v_norm=gmlp_v_norm[l], gmlp_w_s=gmlp_w_s[l], gmlp_b_s=gmlp_b_s[l],
                 w_branch=w_branch[l], w_out=w_out[l], norm_ffn=norm_ffn[l], ffn_w_up=ffn_w_up[l],
                 ffn_conv_w=ffn_conv_w[l], ffn_conv_b=ffn_conv_b[l], ffn_w_down=ffn_w_down[l])
        xp, lat, rop, kk, vv, _, cst = trunk_layer(xp, pos_p, None, w)
        p_lat.append(lat); p_rope.append(rop); p_k.append(kk); p_v.append(vv); p_conv.append(cst)
        past = dict(mla_latent=gather_pages(cache_mla_latent, l, page_table),
                    mla_krope=gather_pages(cache_mla_krope, l, page_table),
                    sb_k=gather_pages(cache_sb_k, l, page_table),
                    sb_v=gather_pages(cache_sb_v, l, page_table),
                    ffn_conv=state_ffn_conv[l])
        xs, lat, rop, kk, vv, gv, cst = trunk_layer(xs, pos_s, past, w)
        s_lat.append(lat); s_rope.append(rop); s_k.append(kk); s_v.append(vv); s_gv.append(gv); s_conv.append(cst)
    y_prompt = rmsnorm(xp, norm_final)
    y_sample = rmsnorm(xs, norm_final)
    return (y_prompt, y_sample,
            jnp.stack(p_lat), jnp.stack(p_rope), jnp.stack(p_k), jnp.stack(p_v), jnp.stack(p_conv),
            jnp.stack(s_lat), jnp.stack(s_rope), jnp.stack(s_k), jnp.stack(s_v), jnp.stack(s_gv), jnp.stack(s_conv))
```

```python
import functools

import jax
import jax.numpy as jnp
import numpy as np
from jax import lax
from jax.experimental import pallas as pl
from jax.experimental.pallas import tpu as pltpu

F32 = jnp.float32
BF16 = jnp.bfloat16

D_MODEL = 1024
N_BRANCH = 3
BRANCH_WIDTH = 512
MLA_HEADS = 8
MLA_NOPE = 64
MLA_ROPE = 32
MLA_V = 64
MLA_Q_LORA = 256
MLA_KV_LORA = 128
MLA_SCALE = (MLA_NOPE + MLA_ROPE) ** -0.5
ROPE_THETA = 10000.0
GMLP_CHUNK = 128
GMLP_GROUPS = 4
GMLP_GROUP_DIM = 128
GMLP_WIDTH = GMLP_GROUPS * GMLP_GROUP_DIM
SB_HEADS = 4
SB_KV_HEADS = 2
SB_GROUP = SB_HEADS // SB_KV_HEADS
SB_HEAD_DIM = 128
SB_SCALE = SB_HEAD_DIM ** -0.5
FFN_DIM = 2816
CONV_W = 3
PAGE_SIZE = 128
EPS = 1e-6
NEG_INF = -1e30

LANES = 128
SUBLANES = 8
VMEM_LIMIT_BYTES = 56 * 1024 * 1024
TOKEN_TILE = 512
ATT_TILE = 256
FFN_CHUNK = 256
N_FFN_CHUNKS = FFN_DIM // FFN_CHUNK
PAGES_PER_STEP = 8
SB_STOP = -105.0

_C_CQ = (0, 256)
_C_CKV = (256, 384)
_C_KR = (384, 512)
_C_GU = (512, 1024)
_C_GV = (1024, 1536)
_C_SQ = (1536, 2048)
_C_SK = (2048, 2304)
_C_SV = (2304, 2560)
W1_COLS = 2560

_NT = (((1,), (1,)), ((), ()))


def _params(grid_rank):
    return pltpu.CompilerParams(vmem_limit_bytes=VMEM_LIMIT_BYTES, dimension_semantics=("arbitrary",) * grid_rank)


def _layer_spec(a, l):
    return pl.BlockSpec((None,) + a.shape[1:], lambda *_: (l,) + (0,) * (a.ndim - 1), pipeline_mode=pl.Buffered(1))


def _rms(x, g):
    return x * lax.rsqrt(jnp.mean(x * x, axis=-1, keepdims=True) + EPS) * g


def _gelu(x):
    return 0.5 * x * (1.0 + jnp.tanh(np.sqrt(2.0 / np.pi).astype(np.float32) * (x + 0.044715 * (x * x * x))))


def _sigmoid(x):
    return 1.0 / (1.0 + jnp.exp(-x))


def _softplus(z):
    return jnp.maximum(z, 0.0) + jnp.log1p(jnp.exp(-jnp.abs(z)))


def _dot(a, b):
    return jnp.dot(a, b, preferred_element_type=F32)


def _dot_nt(a, b):
    return lax.dot_general(a, b, _NT, preferred_element_type=F32)


def _inproj_kernel(x_ref, cos_ref, sin_ref, nmix_ref, w1_ref, qn_ref, wuq_ref, bduk_ref, kvn_ref, gvn_ref,
                   ws_ref, bs_ref, mask_ref,
                   ql_ref, qr_ref, kc_ref, ckv_ref, kr_ref, ogm_ref, gv_ref, sq_ref, skv_ref, sk_ref, sv_ref):
    tm = x_ref.shape[0]
    h = _rms(x_ref[...], nmix_ref[...]).astype(BF16)
    cos_t = cos_ref[...]
    sin_t = sin_ref[...]
    lane = lax.broadcasted_iota(jnp.int32, (1, LANES), 1)
    first_half = (lane % MLA_ROPE) < (MLA_ROPE // 2)

    def rope(v):
        partner = jnp.where(first_half, pltpu.roll(v, LANES - MLA_ROPE // 2, 1), pltpu.roll(v, MLA_ROPE // 2, 1))
        return v * cos_t + partner * sin_t

    def seg(c):
        return _dot(h, w1_ref[:, c[0]:c[1]])

    cq = _rms(seg(_C_CQ), qn_ref[...]).astype(BF16)
    q = _dot(cq, wuq_ref[...])
    n_nope = MLA_HEADS * MLA_NOPE
    ql_ref[...] = (_dot(q[:, :n_nope].astype(BF16), bduk_ref[...]) * MLA_SCALE).astype(BF16)
    for g in range(2):
        qr_ref[:, g * LANES:(g + 1) * LANES] = (
            rope(q[:, n_nope + g * LANES:n_nope + (g + 1) * LANES]) * MLA_SCALE).astype(BF16)
    ckv = _rms(seg(_C_CKV), kvn_ref[...])
    ckv_ref[...] = ckv
    kc_ref[:, 0:LANES] = ckv.astype(BF16)
    kr = rope(seg(_C_KR))
    kr_ref[...] = kr
    kc_ref[:, LANES:2 * LANES] = kr.astype(BF16)

    gu = _gelu(seg(_C_GU))
    gv = _rms(_gelu(seg(_C_GV)), gvn_ref[...])
    gv_ref[...] = gv
    gvb = gv.astype(BF16)
    for g in range(GMLP_GROUPS):
        wsm = (ws_ref[g] * mask_ref[...]).astype(BF16)
        cs = slice(g * GMLP_GROUP_DIM, (g + 1) * GMLP_GROUP_DIM)
        for c in range(tm // GMLP_CHUNK):
            rs = slice(c * GMLP_CHUNK, (c + 1) * GMLP_CHUNK)
            mix = _dot(wsm, gvb[rs, cs]) + bs_ref[g]
            ogm_ref[rs, cs] = (gu[rs, cs] * mix).astype(BF16)

    sq_ref[...] = (seg(_C_SQ) * SB_SCALE).astype(BF16)
    sk = seg(_C_SK)
    sv = seg(_C_SV)
    sk_ref[...] = sk
    sv_ref[...] = sv
    nkv = SB_KV_HEADS * SB_HEAD_DIM
    skv_ref[:, 0:nkv] = sk.astype(BF16)
    skv_ref[:, nkv:2 * nkv] = sv.astype(BF16)


def _inproj(x, cos_t, sin_t, n_pos_tiles, wl, ws, bs, mask, l):
    n = x.shape[0]
    tm = min(TOKEN_TILE, n)
    grid = (n // tm,)
    row = lambda w: pl.BlockSpec((tm, w), lambda i: (i, 0))
    lay = lambda a: _layer_spec(a, l)
    full = lambda a: pl.BlockSpec(a.shape, lambda i: (0,) * a.ndim)
    pos = pl.BlockSpec((tm, LANES), lambda i: (i % n_pos_tiles, 0))
    out_w = [(1024, BF16), (256, BF16), (256, BF16), (128, F32), (128, F32), (512, BF16), (512, F32),
             (512, BF16), (512, BF16), (256, F32), (256, F32)]
    return pl.pallas_call(
        _inproj_kernel,
        grid=grid,
        in_specs=[row(D_MODEL), pos, pos, lay(wl['norm_mix']), lay(wl['w1']), lay(wl['q_norm']), lay(wl['wuq']),
                  lay(wl['bduk']), lay(wl['kv_norm']), lay(wl['gv_norm']), full(ws), full(bs), full(mask)],
        out_specs=[row(w) for w, _ in out_w],
        out_shape=[jax.ShapeDtypeStruct((n, w), d) for w, d in out_w],
        compiler_params=_params(1),
    )(x, cos_t, sin_t, wl['norm_mix'], wl['w1'], wl['q_norm'], wl['wuq'], wl['bduk'], wl['kv_norm'],
      wl['gv_norm'], ws, bs, mask)


def _mla_prompt_kernel(ql_ref, qr_ref, kc_ref, o_ref):
    t = ql_ref.shape[0]
    qi = pl.program_id(1)
    lane = lax.broadcasted_iota(jnp.int32, (1, LANES), 1)
    row = lax.broadcasted_iota(jnp.int32, (t, t), 0)
    col = lax.broadcasted_iota(jnp.int32, (t, t), 1)
    causal = col <= row
    slots = LANES // MLA_ROPE

    for h in range(MLA_HEADS):
        grp = qr_ref[:, (h // slots) * LANES:(h // slots + 1) * LANES]
        q_rope = jnp.where(lane // MLA_ROPE == h % slots, grp, jnp.zeros_like(grp))
        q_h = jnp.concatenate([ql_ref[:, h * LANES:(h + 1) * LANES], q_rope], axis=1)

        def step(k, carry, masked):
            m, l, acc = carry
            s = _dot_nt(q_h, k)
            if masked:
                s = jnp.where(causal, s, NEG_INF)
            m_new = jnp.maximum(m, jnp.max(s, axis=1, keepdims=True))
            p = jnp.exp(s - m_new)
            alpha = jnp.exp(m - m_new)
            l = alpha * l + jnp.sum(p, axis=1, keepdims=True)
            acc = alpha * acc + _dot(p.astype(BF16), k[:, 0:MLA_KV_LORA])
            return m_new, l, acc

        def body(kt, carry):
            k = kc_ref[pl.ds(pl.multiple_of(kt * t, t), t), :]
            return step(k, carry, False)

        init = (jnp.full((t, 1), NEG_INF, F32), jnp.zeros((t, 1), F32), jnp.zeros((t, MLA_KV_LORA), F32))
        carry = lax.fori_loop(0, qi, body, init)
        kd = kc_ref[pl.ds(pl.multiple_of(qi * t, t), t), :]
        m, l, acc = step(kd, carry, True)
        o_ref[:, h * LANES:(h + 1) * LANES] = (acc / l).astype(BF16)


def _mla_prompt(ql, qr, kc, batch, seq):
    t = min(ATT_TILE, seq)
    nq = seq // t
    qspec = lambda w: pl.BlockSpec((t, w), lambda b, i: (b * nq + i, 0))
    return pl.pallas_call(
        _mla_prompt_kernel,
        grid=(batch, nq),
        in_specs=[qspec(1024), qspec(256), pl.BlockSpec((seq, 256), lambda b, i: (b, 0))],
        out_specs=qspec(1024),
        out_shape=jax.ShapeDtypeStruct((batch * seq, 1024), BF16),
        compiler_params=_params(2),
    )(ql, qr, kc)


def _sb_block(q, k, v, u, l_in, visible):
    z = _dot_nt(q, k)
    sp = _softplus(z)
    lk = -sp
    if visible is not None:
        lk = jnp.where(visible, lk, 0.0)
    hi = lk.astype(BF16)
    lo = (lk - hi.astype(F32)).astype(BF16)
    between = _dot(hi, u) + _dot(lo, u)
    a = jnp.exp((z - sp) + (between + l_in))
    if visible is not None:
        a = jnp.where(visible, a, 0.0)
    return _dot(a.astype(BF16), v), l_in + jnp.sum(lk, axis=1, keepdims=True)


def _sb_prompt_kernel(sq_ref, skv_ref, u_ref, o_ref):
    t = sq_ref.shape[0]
    qi = pl.program_id(1)
    row = lax.broadcasted_iota(jnp.int32, (t, t), 0)
    col = lax.broadcasted_iota(jnp.int32, (t, t), 1)
    strict = col < row
    u = u_ref[...]
    d = SB_HEAD_DIM
    nkv = SB_KV_HEADS * d

    for hq in range(SB_HEADS):
        n = hq // SB_GROUP
        q = sq_ref[:, hq * d:(hq + 1) * d]

        def kv(kt):
            rows = pl.ds(pl.multiple_of(kt * t, t), t)
            return skv_ref[rows, n * d:(n + 1) * d], skv_ref[rows, nkv + n * d:nkv + (n + 1) * d]

        k, v = kv(qi)
        acc, l = _sb_block(q, k, v, u, jnp.zeros((t, 1), F32), strict)

        def cond(c):
            return jnp.logical_and(c[0] >= 0, c[1] > SB_STOP)

        def body(c):
            kt, _, acc, l = c
            k, v = kv(kt)
            add, l = _sb_block(q, k, v, u, l, None)
            return kt - 1, jnp.max(l), acc + add, l

        _, _, acc, _ = lax.while_loop(cond, body, (qi - 1, jnp.max(l), acc, l))
        o_ref[:, hq * d:(hq + 1) * d] = acc.astype(BF16)


def _sb_prompt(sq, skv, u, batch, seq):
    t = u.shape[0]
    nq = seq // t
    qspec = pl.BlockSpec((t, 512), lambda b, i: (b * nq + i, 0))
    return pl.pallas_call(
        _sb_prompt_kernel,
        grid=(batch, nq),
        in_specs=[qspec, pl.BlockSpec((seq, 512), lambda b, i: (b, 0)), pl.BlockSpec(u.shape, lambda b, i: (0, 0))],
        out_specs=qspec,
        out_shape=jax.ShapeDtypeStruct((batch * seq, 512), BF16),
        compiler_params=_params(2),
    )(sq, skv, u)


def _mla_sample_kernel(pt_ref, ql_ref, qr_ref, kcn_ref, *rest):
    npg = PAGES_PER_STEP
    lat_refs = rest[:npg]
    kr_refs = rest[npg:2 * npg]
    o_ref = rest[2 * npg]
    qc_s, m_s, l_s, acc_s, kcat_s = rest[2 * npg + 1:]
    j = pl.program_id(1)
    t = ql_ref.shape[1]
    rows = MLA_HEADS * t
    slots = LANES // MLA_ROPE

    @pl.when(j == 0)
    def _():
        ql = ql_ref[0].astype(F32)
        qr = qr_ref[0].astype(F32)
        lane = lax.broadcasted_iota(jnp.int32, (1, LANES), 1)
        q_lat, q_slot, q_std = [], [], []
        for h in range(MLA_HEADS):
            grp = qr[:, (h // slots) * LANES:(h // slots + 1) * LANES]
            own = jnp.where(lane // MLA_ROPE == h % slots, grp, 0.0)
            q_lat.append(ql[:, h * LANES:(h + 1) * LANES])
            q_slot.append(own)
            s = h % slots
            q_std.append(own if s == 0 else pltpu.roll(own, LANES - MLA_ROPE * s, 1))
        q_lat = jnp.concatenate(q_lat, axis=0)
        qc_s[...] = jnp.concatenate([q_lat, jnp.concatenate(q_std, axis=0)], axis=1).astype(BF16)
        qn = jnp.concatenate([q_lat, jnp.concatenate(q_slot, axis=0)], axis=1).astype(BF16)
        kn = kcn_ref[0].astype(F32)
        kn = jnp.concatenate([kn, jnp.zeros((LANES - t, kn.shape[1]), F32)], axis=0).astype(BF16)
        s = _dot_nt(qn, kn)
        tq = lax.broadcasted_iota(jnp.int32, (rows, LANES), 0) % t
        sk = lax.broadcasted_iota(jnp.int32, (rows, LANES), 1)
        s = jnp.where(sk <= tq, s, NEG_INF)
        m = jnp.max(s, axis=1, keepdims=True)
        p = jnp.exp(s - m)
        m_s[...] = m
        l_s[...] = jnp.sum(p, axis=1, keepdims=True)
        acc_s[...] = _dot(p.astype(BF16), kn[:, 0:MLA_KV_LORA])

    kcat_s[:, LANES:2 * LANES] = jnp.zeros((npg * PAGE_SIZE, LANES), BF16)
    for i in range(npg):
        rs = slice(i * PAGE_SIZE, (i + 1) * PAGE_SIZE)
        kcat_s[rs, 0:LANES] = lat_refs[i][...].astype(BF16)
        kcat_s[rs, LANES:LANES + MLA_ROPE] = kr_refs[i][...].astype(BF16)
    kcat = kcat_s[...]
    s = _dot_nt(qc_s[...], kcat)
    m = m_s[...]
    m_new = jnp.maximum(m, jnp.max(s, axis=1, keepdims=True))
    p = jnp.exp(s - m_new)
    alpha = jnp.exp(m - m_new)
    m_s[...] = m_new
    l_s[...] = alpha * l_s[...] + jnp.sum(p, axis=1, keepdims=True)
    acc_s[...] = alpha * acc_s[...] + _dot(p.astype(BF16), kcat[:, 0:MLA_KV_LORA])

    @pl.when(j == pl.num_programs(1) - 1)
    def _():
        o = acc_s[...] / l_s[...]
        for h in range(MLA_HEADS):
            o_ref[0, :, h * LANES:(h + 1) * LANES] = o[h * t:(h + 1) * t, :]


def _mla_sample(pt, ql3, qr3, kc3, cache_lat, cache_kr, l):
    nb, t, _ = ql3.shape
    n_pages = pt.shape[0] // nb
    npg = PAGES_PER_STEP
    steps = n_pages // npg
    seqspec = lambda w: pl.BlockSpec((1, t, w), lambda b, j, pt: (b, 0, 0))

    def page_spec(width, i):
        return pl.BlockSpec((None, None, PAGE_SIZE, width),
                            lambda b, j, pt: (l, pt[b * n_pages + j * npg + i], 0, 0))

    rows = MLA_HEADS * t
    grid_spec = pltpu.PrefetchScalarGridSpec(
        num_scalar_prefetch=1,
        grid=(nb, steps),
        in_specs=[seqspec(1024), seqspec(256), seqspec(256)]
        + [page_spec(MLA_KV_LORA, i) for i in range(npg)] + [page_spec(MLA_ROPE, i) for i in range(npg)],
        out_specs=seqspec(1024),
        scratch_shapes=[pltpu.VMEM((rows, 2 * LANES), BF16), pltpu.VMEM((rows, 1), F32), pltpu.VMEM((rows, 1), F32),
                        pltpu.VMEM((rows, MLA_KV_LORA), F32), pltpu.VMEM((npg * PAGE_SIZE, 2 * LANES), BF16)],
    )
    return pl.pallas_call(
        _mla_sample_kernel,
        grid_spec=grid_spec,
        out_shape=jax.ShapeDtypeStruct((nb, t, 1024), F32),
        compiler_params=_params(2),
    )(pt, ql3, qr3, kc3, *([cache_lat] * npg), *([cache_kr] * npg))


def _sb_sample_kernel(pt_ref, sq_ref, kn_ref, vn_ref, u_ref, *rest):
    npg = PAGES_PER_STEP
    k_refs = rest[:npg]
    v_refs = rest[npg:2 * npg]
    o_ref = rest[2 * npg]
    q_s, l_s, acc_s, done_s = rest[2 * npg + 1:]
    j = pl.program_id(1)
    t = sq_ref.shape[1]
    d = SB_HEAD_DIM
    rows = SB_HEADS * t
    nk = SB_KV_HEADS * PAGE_SIZE
    u = u_ref[...]
    row_head = lax.broadcasted_iota(jnp.int32, (rows, nk), 0) // (SB_GROUP * t)
    col = lax.broadcasted_iota(jnp.int32, (rows, nk), 1)
    same_head = (col % SB_KV_HEADS) == row_head

    @pl.when(j == 0)
    def _():
        sq = sq_ref[0].astype(F32)
        q = jnp.concatenate([sq[:, hq * d:(hq + 1) * d] for hq in range(SB_HEADS)], axis=0).astype(BF16)
        q_s[...] = q
        pad = jnp.zeros((nk - SB_KV_HEADS * t, d), F32)
        kn = jnp.concatenate([kn_ref[0], pad], axis=0).astype(BF16)
        vn = jnp.concatenate([vn_ref[0], pad], axis=0).astype(BF16)
        tq = lax.broadcasted_iota(jnp.int32, (rows, nk), 0) % t
        vis = jnp.logical_and(same_head, (col // SB_KV_HEADS) < tq)
        acc, l = _sb_block(q, kn, vn, u, jnp.zeros((rows, 1), F32), vis)
        acc_s[...] = acc
        l_s[...] = l
        done_s[0] = (jnp.max(l) <= SB_STOP).astype(jnp.int32)

    @pl.when(done_s[0] == 0)
    def _():
        q = q_s[...]
        acc = acc_s[...]
        l = l_s[...]
        for i in range(npg):
            add, l = _sb_block(q, k_refs[i][...].astype(BF16), v_refs[i][...].astype(BF16), u, l, same_head)
            acc = acc + add
        acc_s[...] = acc
        l_s[...] = l
        done_s[0] = (jnp.max(l) <= SB_STOP).astype(jnp.int32)

    @pl.when(j == pl.num_programs(1) - 1)
    def _():
        acc = acc_s[...]
        for hq in range(SB_HEADS):
            o_ref[0, :, hq * d:(hq + 1) * d] = acc[hq * t:(hq + 1) * t, :]


def _sb_sample(pt, sq3, kn3, vn3, u, cache_k, cache_v, l):
    nb, t, _ = sq3.shape
    n_pages = pt.shape[0] // nb
    npg = PAGES_PER_STEP
    steps = n_pages // npg
    nk = SB_KV_HEADS * PAGE_SIZE
    rows = SB_HEADS * t

    def page_spec(i):
        return pl.BlockSpec((None, None, nk, SB_HEAD_DIM),
                            lambda b, j, pt: (l, pt[b * n_pages + n_pages - 1 - (j * npg + i)], 0, 0))

    grid_spec = pltpu.PrefetchScalarGridSpec(
        num_scalar_prefetch=1,
        grid=(nb, steps),
        in_specs=[pl.BlockSpec((1, t, 512), lambda b, j, pt: (b, 0, 0)),
                  pl.BlockSpec((1, SB_KV_HEADS * t, SB_HEAD_DIM), lambda b, j, pt: (b, 0, 0)),
                  pl.BlockSpec((1, SB_KV_HEADS * t, SB_HEAD_DIM), lambda b, j, pt: (b, 0, 0)),
                  pl.BlockSpec(u.shape, lambda b, j, pt: (0, 0))]
        + [page_spec(i) for i in range(npg)] + [page_spec(i) for i in range(npg)],
        out_specs=pl.BlockSpec((1, t, 512), lambda b, j, pt: (b, 0, 0)),
        scratch_shapes=[pltpu.VMEM((rows, SB_HEAD_DIM), BF16), pltpu.VMEM((rows, 1), F32),
                        pltpu.VMEM((rows, SB_HEAD_DIM), F32), pltpu.SMEM((1,), jnp.int32)],
    )
    return pl.pallas_call(
        _sb_sample_kernel,
        grid_spec=grid_spec,
        out_shape=jax.ShapeDtypeStruct((nb, t, 512), F32),
        compiler_params=_params(2),
    )(pt, sq3, kn3, vn3, u, *([cache_k] * npg), *([cache_v] * npg))


def _merge_kernel(x_ref, olat_ref, ogm_ref, osb_ref, nmix_ref, wg_ref, bduv_ref, wbr_ref, wout_ref, x1_ref):
    x = x_ref[...]
    h = _rms(x, nmix_ref[...]).astype(BF16)
    o_mla = _dot(olat_ref[...].astype(BF16), bduv_ref[...]).astype(BF16)
    branches = (o_mla, ogm_ref[...], osb_ref[...].astype(BF16))
    merged = None
    for b in range(N_BRANCH):
        gate = _sigmoid(_dot(h, wg_ref[:, b * D_MODEL:(b + 1) * D_MODEL]))
        term = gate * _dot(branches[b], wbr_ref[b])
        merged = term if merged is None else merged + term
    x1_ref[...] = x + _dot(merged.astype(BF16), wout_ref[...])


def _merge(x, olat, ogm, osb, wl, l):
    n = x.shape[0]
    tm = min(TOKEN_TILE, n)
    row = lambda w: pl.BlockSpec((tm, w), lambda i: (i, 0))
    lay = lambda a: _layer_spec(a, l)
    ws = [wl['norm_mix'], wl['wg'], wl['bduv'], wl['wbr'], wl['wout']]
    return pl.pallas_call(
        _merge_kernel,
        grid=(n // tm,),
        in_specs=[row(D_MODEL), row(1024), row(512), row(512)] + [lay(a) for a in ws],
        out_specs=row(D_MODEL),
        out_shape=jax.ShapeDtypeStruct((n, D_MODEL), F32),
        compiler_params=_params(1),
    )(x, olat, ogm, osb, *ws)


def _ffn_kernel(*refs, sample, final, tiles_per_seq, t_new):
    x1_ref, nffn_ref, wup_ref, cw_ref, wdn_ref = refs[:5]
    pos = 5
    if sample:
        st_ref = refs[pos]
        pos += 1
    if final:
        nfin_ref = refs[pos]
        pos += 1
    x2_ref, cst_ref = refs[pos:pos + 2]
    pos += 2
    acc_s, h2_s = refs[pos:pos + 2]
    if not sample:
        carry_s = refs[pos + 2]
    tm = x1_ref.shape[0]
    i = pl.program_id(0)
    x1 = x1_ref[...]
    h2_s[...] = _rms(x1, nffn_ref[...]).astype(BF16)
    acc_s[...] = jnp.zeros_like(acc_s)
    row = lax.broadcasted_iota(jnp.int32, (tm, 1), 0)

    if not sample:
        @pl.when(i % tiles_per_seq == 0)
        def _():
            carry_s[...] = jnp.zeros_like(carry_s)

    def chunk(c, _):
        up = _dot(h2_s[...], wup_ref[c])
        r1 = pltpu.roll(up, 1, 0)
        r2 = pltpu.roll(up, 2, 0)
        if sample:
            nseq = tm // t_new
            rep = lambda a: jnp.broadcast_to(a[:, None, :], (nseq, t_new, a.shape[-1])).reshape(tm, a.shape[-1])
            p0 = rep(st_ref[0, c])
            p1 = rep(st_ref[1, c])
            rm = row % t_new
            cst_ref[c] = up
        else:
            p0 = carry_s[c, SUBLANES - 2:SUBLANES - 1, :]
            p1 = carry_s[c, SUBLANES - 1:SUBLANES, :]
            rm = row
            tail = up[tm - SUBLANES:tm, :]
            carry_s[c] = tail
            cst_ref[0, c] = tail
        m1 = jnp.where(rm == 0, p1, r1)
        m2 = jnp.where(rm == 0, p0, jnp.where(rm == 1, p1, r2))
        cw = cw_ref[c]
        conv = cw[3:4, :] + cw[2:3, :] * up
        conv = conv + cw[0:1, :] * m2
        conv = conv + cw[1:2, :] * m1
        gate = conv[:, 0:FFN_CHUNK]
        val = conv[:, FFN_CHUNK:2 * FFN_CHUNK]
        act = (gate * _sigmoid(gate) * val).astype(BF16)
        acc_s[...] += _dot(act, wdn_ref[c])
        return 0

    lax.fori_loop(0, N_FFN_CHUNKS, chunk, 0)
    x2 = x1 + acc_s[...]
    if final:
        x2 = _rms(x2, nfin_ref[...])
    x2_ref[...] = x2


def _ffn(x1, wl, l, nfin, state, batch, seq, final):
    n = x1.shape[0]
    sample = state is not None
    tm = min(TOKEN_TILE // 2, n) if sample else min(TOKEN_TILE, seq)
    grid = (n // tm,)
    row = lambda w: pl.BlockSpec((tm, w), lambda i: (i, 0))
    lay = lambda a: _layer_spec(a, l)
    ws = [wl['norm_ffn'], wl['wup'], wl['cw'], wl['wdn']]
    args = [x1] + ws
    in_specs = [row(D_MODEL)] + [lay(a) for a in ws]
    c2 = 2 * FFN_CHUNK
    if sample:
        t_new = seq
        args.append(state)
        in_specs.append(pl.BlockSpec((2, N_FFN_CHUNKS, tm // t_new, c2), lambda i: (0, 0, i, 0)))
        cst_shape = jax.ShapeDtypeStruct((N_FFN_CHUNKS, n, c2), F32)
        cst_spec = pl.BlockSpec((N_FFN_CHUNKS, tm, c2), lambda i: (0, i, 0))
        tiles_per_seq = 1
    else:
        t_new = 1
        tiles_per_seq = seq // tm
        cst_shape = jax.ShapeDtypeStruct((batch, N_FFN_CHUNKS, SUBLANES, c2), F32)
        cst_spec = pl.BlockSpec((1, N_FFN_CHUNKS, SUBLANES, c2), lambda i: (i // tiles_per_seq, 0, 0, 0))
    if final:
        args.append(nfin)
        in_specs.append(pl.BlockSpec(nfin.shape, lambda i: (0, 0)))
    scratch = [pltpu.VMEM((tm, D_MODEL), F32), pltpu.VMEM((tm, D_MODEL), BF16)]
    if not sample:
        scratch.append(pltpu.VMEM((N_FFN_CHUNKS, SUBLANES, c2), F32))
    return pl.pallas_call(
        functools.partial(_ffn_kernel, sample=sample, final=final, tiles_per_seq=tiles_per_seq, t_new=t_new),
        grid=grid,
        in_specs=in_specs,
        out_specs=[row(D_MODEL), cst_spec],
        out_shape=[jax.ShapeDtypeStruct((n, D_MODEL), F32), cst_shape],
        scratch_shapes=scratch,
        compiler_params=_params(1),
    )(*args)


def _block_diag(blocks):
    dpt, n, r, c = blocks.shape
    out = jnp.zeros((dpt, n, r, n, c), blocks.dtype)
    idx = jnp.arange(n)
    out = out.at[:, idx, :, idx, :].set(jnp.moveaxis(blocks, 1, 0))
    return out.reshape(dpt, n * r, n * c)


def _chunk_cols(a):
    lead = a.shape[:-1]
    g = a[..., :FFN_DIM].reshape(lead + (N_FFN_CHUNKS, FFN_CHUNK))
    v = a[..., FFN_DIM:].reshape(lead + (N_FFN_CHUNKS, FFN_CHUNK))
    return jnp.concatenate([g, v], axis=-1)


def _unchunk_cols(a):
    lead = a.shape[:-2]
    g = a[..., :FFN_CHUNK].reshape(lead + (FFN_DIM,))
    v = a[..., FFN_CHUNK:].reshape(lead + (FFN_DIM,))
    return jnp.concatenate([g, v], axis=-1)


def _rope_tables(pos):
    half = MLA_ROPE // 2
    inv_freq = ROPE_THETA ** (-jnp.arange(half, dtype=jnp.float32) / half)
    ang = pos.astype(jnp.float32)[:, None] * inv_freq
    cos, sin = jnp.cos(ang), jnp.sin(ang)
    reps = LANES // MLA_ROPE
    cos_t = jnp.tile(jnp.concatenate([cos, cos], axis=1), (1, reps))
    sin_t = jnp.tile(jnp.concatenate([-sin, sin], axis=1), (1, reps))
    return cos_t, sin_t


def _prep_weights(norm_mix, w_in, mla_q_norm, mla_w_uq, mla_kv_norm, mla_w_uk, mla_w_uv, gmlp_v_norm,
                  w_branch, w_out, norm_ffn, ffn_w_up, ffn_conv_w, ffn_conv_b, ffn_w_down):
    depth = w_in.shape[0]
    o = np.cumsum((0, MLA_Q_LORA, MLA_KV_LORA, MLA_ROPE, GMLP_WIDTH, GMLP_WIDTH, SB_HEADS * SB_HEAD_DIM,
                   SB_KV_HEADS * SB_HEAD_DIM, SB_KV_HEADS * SB_HEAD_DIM, N_BRANCH * D_MODEL)).tolist()
    w1 = jnp.concatenate([w_in[:, :, o[0]:o[2]], jnp.tile(w_in[:, :, o[2]:o[3]], (1, 1, LANES // MLA_ROPE)),
                          w_in[:, :, o[3]:o[8]]], axis=2).astype(BF16)
    wuq3 = mla_w_uq.reshape(depth, MLA_Q_LORA, MLA_HEADS, MLA_NOPE + MLA_ROPE)
    wuq = jnp.concatenate([wuq3[..., :MLA_NOPE].reshape(depth, MLA_Q_LORA, -1),
                           wuq3[..., MLA_NOPE:].reshape(depth, MLA_Q_LORA, -1)], axis=2).astype(BF16)
    cw = jnp.concatenate([_chunk_cols(ffn_conv_w), _chunk_cols(ffn_conv_b)[:, None],
                          jnp.zeros((depth, SUBLANES - CONV_W - 1, N_FFN_CHUNKS, 2 * FFN_CHUNK), F32)], axis=1)
    r3 = lambda a: a.reshape(depth, 1, -1)
    return dict(
        norm_mix=r3(norm_mix), w1=w1, wg=w_in[:, :, o[8]:o[9]].astype(BF16),
        q_norm=r3(mla_q_norm), wuq=wuq, kv_norm=r3(mla_kv_norm), gv_norm=r3(gmlp_v_norm),
        bduk=_block_diag(jnp.transpose(mla_w_uk, (0, 2, 3, 1))).astype(BF16),
        bduv=_block_diag(jnp.transpose(mla_w_uv, (0, 2, 1, 3))).astype(BF16),
        wbr=w_branch.astype(BF16), wout=w_out.astype(BF16), norm_ffn=r3(norm_ffn),
        wup=jnp.moveaxis(_chunk_cols(ffn_w_up), 2, 1).astype(BF16),
        cw=jnp.moveaxis(cw, 2, 1),
        wdn=ffn_w_down.reshape(depth, N_FFN_CHUNKS, FFN_CHUNK, D_MODEL).astype(BF16),
    )


def kernel(x_prompt, x_sample, cache_mla_latent, cache_mla_krope, cache_sb_k, cache_sb_v, state_ffn_conv, page_table, norm_mix, w_in, mla_q_norm, mla_w_uq, mla_kv_norm, mla_w_uk, mla_w_uv, gmlp_v_norm, gmlp_w_s, gmlp_b_s, w_branch, w_out, norm_ffn, ffn_w_up, ffn_conv_w, ffn_conv_b, ffn_w_down, norm_final):
    batch, seq, _ = x_prompt.shape
    nb, t_new, _ = x_sample.shape
    depth = w_in.shape[0]
    n_pool = cache_sb_k.shape[1]
    past_len = page_table.shape[1] * PAGE_SIZE
    assert t_new == SUBLANES and LANES % t_new == 0
    assert seq % ATT_TILE == 0 or seq < ATT_TILE

    wl = _prep_weights(norm_mix, w_in, mla_q_norm, mla_w_uq, mla_kv_norm, mla_w_uk, mla_w_uv, gmlp_v_norm,
                       w_branch, w_out, norm_ffn, ffn_w_up, ffn_conv_w, ffn_conv_b, ffn_w_down)
    nfin = norm_final.reshape(1, D_MODEL)

    n_s = nb * t_new
    tm_s = min(TOKEN_TILE, n_s)
    cos_p, sin_p = _rope_tables(jnp.arange(seq, dtype=jnp.int32))
    cos_s, sin_s = _rope_tables(jnp.tile(past_len + jnp.arange(t_new, dtype=jnp.int32), tm_s // t_new))
    tm_p = min(TOKEN_TILE, batch * seq)
    pos_tiles_p = max(seq // tm_p, 1)

    r = jnp.arange(GMLP_CHUNK)
    mask_p = (r[None, :] <= r[:, None]).astype(F32)
    mask_s = jnp.logical_and(r[None, :] // t_new == r[:, None] // t_new, r[None, :] <= r[:, None]).astype(F32)
    reps = GMLP_CHUNK // t_new
    ws_s = jnp.tile(gmlp_w_s[:, :, :t_new, :t_new], (1, 1, reps, reps))
    bs_p = jnp.broadcast_to(gmlp_b_s[..., None], gmlp_b_s.shape + (GMLP_GROUP_DIM,))
    bs_s = jnp.broadcast_to(jnp.tile(gmlp_b_s[:, :, :t_new], (1, 1, reps))[..., None], bs_p.shape)

    ta = min(ATT_TILE, seq)
    ra = jnp.arange(ta)
    u_p = (ra[:, None] > ra[None, :]).astype(BF16)
    rk = jnp.arange(SB_KV_HEADS * PAGE_SIZE)
    u_s = (rk[:, None] > rk[None, :]).astype(BF16)

    pt = page_table.reshape(-1)
    cache_k2 = cache_sb_k.reshape(depth, n_pool, PAGE_SIZE * SB_KV_HEADS, SB_HEAD_DIM)
    cache_v2 = cache_sb_v.reshape(depth, n_pool, PAGE_SIZE * SB_KV_HEADS, SB_HEAD_DIM)
    state_c = jnp.moveaxis(_chunk_cols(state_ffn_conv), (2, 3), (1, 2))

    xp = x_prompt.reshape(batch * seq, D_MODEL)
    xs = x_sample.reshape(n_s, D_MODEL)
    outs = [[] for _ in range(11)]
    for l in range(depth):
        last = l == depth - 1
        ql, qr, kc, ckv, kr, ogm, _, sq, skv, sk, sv = _inproj(
            xp, cos_p, sin_p, pos_tiles_p, wl, gmlp_w_s[l], bs_p[l], mask_p, l)
        olat = _mla_prompt(ql, qr, kc, batch, seq)
        osb = _sb_prompt(sq, skv, u_p, batch, seq)
        x1 = _merge(xp, olat, ogm, osb, wl, l)
        xp, cst = _ffn(x1, wl, l, nfin, None, batch, seq, last)
        outs[0].append(ckv.reshape(batch, seq, MLA_KV_LORA))
        outs[1].append(kr[:, :MLA_ROPE].reshape(batch, seq, MLA_ROPE))
        outs[2].append(sk.reshape(batch, seq, SB_KV_HEADS, SB_HEAD_DIM))
        outs[3].append(sv.reshape(batch, seq, SB_KV_HEADS, SB_HEAD_DIM))
        outs[4].append(_unchunk_cols(jnp.swapaxes(cst[:, :, SUBLANES - 2:, :], 1, 2)))
        ql, qr, kc, ckv, kr, ogm, gv, sq, skv, sk, sv = _inproj(
            xs, cos_s, sin_s, 1, wl, ws_s[l], bs_s[l], mask_s, l)
        r3 = lambda a: a.reshape(nb, t_new, a.shape[-1])
        olat = _mla_sample(pt, r3(ql), r3(qr), r3(kc), cache_mla_latent, cache_mla_krope, l)
        osb = _sb_sample(pt, r3(sq), sk.reshape(nb, t_new * SB_KV_HEADS, SB_HEAD_DIM),
                         sv.reshape(nb, t_new * SB_KV_HEADS, SB_HEAD_DIM), u_s, cache_k2, cache_v2, l)
        x1 = _merge(xs, olat.reshape(n_s, -1), ogm, osb.reshape(n_s, -1), wl, l)
        xs, upf = _ffn(x1, wl, l, nfin, state_c[l], nb, t_new, last)
        outs[5].append(r3(ckv))
        outs[6].append(r3(kr[:, :MLA_ROPE]))
        outs[7].append(sk.reshape(nb, t_new, SB_KV_HEADS, SB_HEAD_DIM))
        outs[8].append(sv.reshape(nb, t_new, SB_KV_HEADS, SB_HEAD_DIM))
        outs[9].append(r3(gv))
        tail = upf.reshape(N_FFN_CHUNKS, nb, t_new, 2 * FFN_CHUNK)[:, :, t_new - (CONV_W - 1):, :]
        outs[10].append(_unchunk_cols(jnp.transpose(tail, (1, 2, 0, 3))))
    return (xp.reshape(batch, seq, D_MODEL), xs.reshape(nb, t_new, D_MODEL)) + tuple(jnp.stack(o) for o in outs)
```

```python
import functools

import jax
import jax.numpy as jnp
import numpy as np
from jax import lax
from jax.experimental import pallas as pl
from jax.experimental.pallas import tpu as pltpu

F32 = jnp.float32
BF16 = jnp.bfloat16

D_MODEL = 1024
N_BRANCH = 3
BRANCH_WIDTH = 512
MLA_HEADS = 8
MLA_NOPE = 64
MLA_ROPE = 32
MLA_V = 64
MLA_Q_LORA = 256
MLA_KV_LORA = 128
MLA_SCALE = (MLA_NOPE + MLA_ROPE) ** -0.5
ROPE_THETA = 10000.0
GMLP_CHUNK = 128
GMLP_GROUPS = 4
GMLP_GROUP_DIM = 128
GMLP_WIDTH = GMLP_GROUPS * GMLP_GROUP_DIM
SB_HEADS = 4
SB_KV_HEADS = 2
SB_GROUP = SB_HEADS // SB_KV_HEADS
SB_HEAD_DIM = 128
SB_SCALE = SB_HEAD_DIM ** -0.5
FFN_DIM = 2816
CONV_W = 3
PAGE_SIZE = 128
EPS = 1e-6
NEG_INF = -1e30

LANES = 128
SUBLANES = 8
VMEM_LIMIT_BYTES = 56 * 1024 * 1024
TOKEN_TILE = 512
ATT_TILE = 256
FFN_CHUNK = 256
N_FFN_CHUNKS = FFN_DIM // FFN_CHUNK
MLA_PAGES_PER_STEP = 32
SB_PAGES_AHEAD = 2
MLA_QK_AHEAD = 8
LOG2E = 1.4426950408889634
SB_STOP = -105.0

_C_CQ = (0, 256)
_C_CKV = (256, 384)
_C_KR = (384, 512)
_C_GU = (512, 1024)
_C_GV = (1024, 1536)
_C_SQ = (1536, 2048)
_C_SK = (2048, 2304)
_C_SV = (2304, 2560)
W1_COLS = 2560

_NT = (((1,), (1,)), ((), ()))


def _params(grid_rank):
    return pltpu.CompilerParams(vmem_limit_bytes=VMEM_LIMIT_BYTES, dimension_semantics=("arbitrary",) * grid_rank)


def _layer_spec(a, l):
    return pl.BlockSpec((None,) + a.shape[1:], lambda *_: (l,) + (0,) * (a.ndim - 1), pipeline_mode=pl.Buffered(1))


def _rms(x, g):
    return x * lax.rsqrt(jnp.mean(x * x, axis=-1, keepdims=True) + EPS) * g


def _gelu(x):
    return 0.5 * x * (1.0 + jnp.tanh(np.sqrt(2.0 / np.pi).astype(np.float32) * (x + 0.044715 * (x * x * x))))


def _sigmoid(x):
    return 1.0 / (1.0 + jnp.exp(-x))


def _softplus(z):
    return jnp.maximum(z, 0.0) + jnp.log(1.0 + jnp.exp(-jnp.abs(z)))


def _dot(a, b):
    return jnp.dot(a, b, preferred_element_type=F32)


def _dot_nt(a, b):
    return lax.dot_general(a, b, _NT, preferred_element_type=F32)


def _inproj_kernel(x_ref, cos_ref, sin_ref, nmix_ref, w1_ref, qn_ref, wuq_ref, bduk_ref, kvn_ref, gvn_ref,
                   ws_ref, bs_ref, mask_ref,
                   ql_ref, qr_ref, kc_ref, ckv_ref, kr_ref, ogm_ref, gv_ref, sq_ref, skv_ref, sk_ref, sv_ref):
    tm = x_ref.shape[0]
    h = _rms(x_ref[...], nmix_ref[...]).astype(BF16)
    cos_t = cos_ref[...]
    sin_t = sin_ref[...]
    lane = lax.broadcasted_iota(jnp.int32, (1, LANES), 1)
    first_half = (lane % MLA_ROPE) < (MLA_ROPE // 2)

    def rope(v):
        partner = jnp.where(first_half, pltpu.roll(v, LANES - MLA_ROPE // 2, 1), pltpu.roll(v, MLA_ROPE // 2, 1))
        return v * cos_t + partner * sin_t

    def seg(c):
        return _dot(h, w1_ref[:, c[0]:c[1]])

    cq = _rms(seg(_C_CQ), qn_ref[...]).astype(BF16)
    q = _dot(cq, wuq_ref[...])
    n_nope = MLA_HEADS * MLA_NOPE
    ql_ref[...] = (_dot(q[:, :n_nope].astype(BF16), bduk_ref[...]) * (MLA_SCALE * LOG2E)).astype(BF16)
    for g in range(2):
        qr_ref[:, g * LANES:(g + 1) * LANES] = (
            rope(q[:, n_nope + g * LANES:n_nope + (g + 1) * LANES]) * (MLA_SCALE * LOG2E)).astype(BF16)
    ckv = _rms(seg(_C_CKV), kvn_ref[...])
    ckv_ref[...] = ckv
    kc_ref[:, 0:LANES] = ckv.astype(BF16)
    kr = rope(seg(_C_KR))
    kr_ref[...] = kr
    kc_ref[:, LANES:2 * LANES] = kr.astype(BF16)

    gu = _gelu(seg(_C_GU))
    gv = _rms(_gelu(seg(_C_GV)), gvn_ref[...])
    gv_ref[...] = gv
    gvb = gv.astype(BF16)
    for g in range(GMLP_GROUPS):
        wsm = (ws_ref[g] * mask_ref[...]).astype(BF16)
        cs = slice(g * GMLP_GROUP_DIM, (g + 1) * GMLP_GROUP_DIM)
        for c in range(tm // GMLP_CHUNK):
            rs = slice(c * GMLP_CHUNK, (c + 1) * GMLP_CHUNK)
            mix = _dot(wsm, gvb[rs, cs]) + bs_ref[g]
            ogm_ref[rs, cs] = (gu[rs, cs] * mix).astype(BF16)

    sq_ref[...] = (seg(_C_SQ) * SB_SCALE).astype(BF16)
    sk = seg(_C_SK)
    sv = seg(_C_SV)
    sk_ref[...] = sk
    sv_ref[...] = sv
    nkv = SB_KV_HEADS * SB_HEAD_DIM
    skv_ref[:, 0:nkv] = sk.astype(BF16)
    skv_ref[:, nkv:2 * nkv] = sv.astype(BF16)


def _inproj(x, cos_t, sin_t, n_pos_tiles, wl, ws, bs, mask, l):
    n = x.shape[0]
    tm = min(TOKEN_TILE, n)
    grid = (n // tm,)
    row = lambda w: pl.BlockSpec((tm, w), lambda i: (i, 0))
    lay = lambda a: _layer_spec(a, l)
    full = lambda a: pl.BlockSpec(a.shape, lambda i: (0,) * a.ndim)
    pos = pl.BlockSpec((tm, LANES), lambda i: (i % n_pos_tiles, 0))
    out_w = [(1024, BF16), (256, BF16), (256, BF16), (128, F32), (128, F32), (512, BF16), (512, F32),
             (512, BF16), (512, BF16), (256, F32), (256, F32)]
    return pl.pallas_call(
        _inproj_kernel,
        grid=grid,
        in_specs=[row(D_MODEL), pos, pos, lay(wl['norm_mix']), lay(wl['w1']), lay(wl['q_norm']), lay(wl['wuq']),
                  lay(wl['bduk']), lay(wl['kv_norm']), lay(wl['gv_norm']), full(ws), full(bs), full(mask)],
        out_specs=[row(w) for w, _ in out_w],
        out_shape=[jax.ShapeDtypeStruct((n, w), d) for w, d in out_w],
        compiler_params=_params(1),
    )(x, cos_t, sin_t, wl['norm_mix'], wl['w1'], wl['q_norm'], wl['wuq'], wl['bduk'], wl['kv_norm'],
      wl['gv_norm'], ws, bs, mask)


def _transpose_bf16(x):
    return x.astype(F32).T.astype(BF16)


def _mla_prompt_kernel(ql_ref, qr_ref, kc_ref, o_ref, vt_s, qt_s, m_s, l_s, acc_s):
    t = ql_ref.shape[0]
    qi = pl.program_id(1)
    slots = LANES // MLA_ROPE

    @pl.when(qi == 0)
    def _():
        def tr(kt, _):
            rows = pl.ds(pl.multiple_of(kt * t, t), t)
            vt_s[kt] = _transpose_bf16(kc_ref[rows, 0:MLA_KV_LORA])
            return 0
        lax.fori_loop(0, vt_s.shape[0], tr, 0)

    sub = lax.broadcasted_iota(jnp.int32, (LANES, 1), 0)
    grp_t = [qr_ref[:, g * LANES:(g + 1) * LANES].astype(F32).T for g in range(MLA_HEADS // slots)]
    for h in range(MLA_HEADS):
        cs = slice(h * t, (h + 1) * t)
        qt_s[0:LANES, cs] = _transpose_bf16(ql_ref[:, h * LANES:(h + 1) * LANES])
        qt_s[LANES:2 * LANES, cs] = jnp.where(sub // MLA_ROPE == h % slots, grp_t[h // slots], 0.0).astype(BF16)
    m_s[...] = jnp.full(m_s.shape, NEG_INF, F32)
    l_s[...] = jnp.zeros(l_s.shape, F32)
    acc_s[...] = jnp.zeros(acc_s.shape, F32)

    key = lax.broadcasted_iota(jnp.int32, (t, t), 0)
    qry = lax.broadcasted_iota(jnp.int32, (t, t), 1)
    causal = key <= qry

    def tile(kt, masked):
        k = kc_ref[pl.ds(pl.multiple_of(kt * t, t), t), :]
        vt = vt_s[kt]

        def scores(h):
            return _dot(k, qt_s[:, h * t:(h + 1) * t])

        s_all = [scores(h) for h in range(MLA_QK_AHEAD)]
        for h in range(MLA_HEADS):
            cs = slice(h * t, (h + 1) * t)
            s = s_all[h]
            if masked:
                s = jnp.where(causal, s, NEG_INF)
            m = m_s[:, cs]
            m_new = jnp.maximum(m, jnp.max(s, axis=0, keepdims=True))
            p = jnp.exp2(s - m_new)
            alpha = jnp.exp2(m - m_new)
            m_s[:, cs] = m_new
            l_s[:, cs] = alpha * l_s[:, cs] + jnp.sum(p, axis=0, keepdims=True)
            if h + MLA_QK_AHEAD < MLA_HEADS:
                s_all.append(scores(h + MLA_QK_AHEAD))
            acc_s[:, cs] = alpha * acc_s[:, cs] + _dot(vt, p.astype(BF16))

    def body(kt, _):
        tile(kt, False)
        return 0

    lax.fori_loop(0, qi, body, 0)
    tile(qi, True)
    for h in range(MLA_HEADS):
        cs = slice(h * t, (h + 1) * t)
        o = acc_s[:, cs] * (1.0 / l_s[:, cs])
        o_ref[:, h * LANES:(h + 1) * LANES] = o.T.astype(BF16)


def _mla_prompt(ql, qr, kc, batch, seq):
    t = min(ATT_TILE, seq)
    nq = seq // t
    qspec = lambda w: pl.BlockSpec((t, w), lambda b, i: (b * nq + i, 0))
    return pl.pallas_call(
        _mla_prompt_kernel,
        grid=(batch, nq),
        in_specs=[qspec(1024), qspec(256), pl.BlockSpec((seq, 256), lambda b, i: (b, 0))],
        out_specs=qspec(1024),
        out_shape=jax.ShapeDtypeStruct((batch * seq, 1024), BF16),
        scratch_shapes=[pltpu.VMEM((nq, MLA_KV_LORA, t), BF16), pltpu.VMEM((2 * LANES, MLA_HEADS * t), BF16),
                        pltpu.VMEM((1, MLA_HEADS * t), F32), pltpu.VMEM((1, MLA_HEADS * t), F32),
                        pltpu.VMEM((MLA_KV_LORA, MLA_HEADS * t), F32)],
        compiler_params=_params(2),
    )(ql, qr, kc)


def _sb_block(q, k, v, u, l_in, visible):
    z = _dot_nt(q, k)
    sp = _softplus(z)
    lk = -sp
    if visible is not None:
        lk = jnp.where(visible, lk, 0.0)
    hi = lk.astype(BF16)
    lo = (lk - hi.astype(F32)).astype(BF16)
    between = _dot(hi, u) + _dot(lo, u)
    a = jnp.exp((z - sp) + (between + l_in))
    if visible is not None:
        a = jnp.where(visible, a, 0.0)
    return _dot(a.astype(BF16), v), l_in + jnp.sum(lk, axis=1, keepdims=True)


def _sb_prompt_kernel(sq_ref, skv_ref, ut_ref, o_ref, vt_s, qt_s, l_s, acc_s):
    t = sq_ref.shape[0]
    qi = pl.program_id(1)
    d = SB_HEAD_DIM
    nkv = SB_KV_HEADS * d
    w = SB_GROUP * t

    @pl.when(qi == 0)
    def _():
        def tr(kt, _):
            rows = pl.ds(pl.multiple_of(kt * t, t), t)
            vt_s[kt] = _transpose_bf16(skv_ref[rows, nkv:2 * nkv])
            return 0
        lax.fori_loop(0, vt_s.shape[0], tr, 0)

    for hq in range(SB_HEADS):
        qt_s[:, hq * t:(hq + 1) * t] = _transpose_bf16(sq_ref[:, hq * d:(hq + 1) * d])
    l_s[...] = jnp.zeros(l_s.shape, F32)
    acc_s[...] = jnp.zeros(acc_s.shape, F32)
    ut = ut_ref[...]
    key = lax.broadcasted_iota(jnp.int32, (t, w), 0)
    qry = lax.broadcasted_iota(jnp.int32, (t, w), 1) % t
    strict = key < qry

    def tile(kt, masked):
        rows = pl.ds(pl.multiple_of(kt * t, t), t)
        vt = vt_s[kt]
        heads = range(SB_KV_HEADS)
        cols = [slice(n * w, (n + 1) * w) for n in heads]
        z = [_dot(skv_ref[rows, n * d:(n + 1) * d], qt_s[:, cols[n]]) for n in heads]
        sp = [_softplus(z[n]) for n in heads]
        lk = [jnp.where(strict, -sp[n], 0.0) if masked else -sp[n] for n in heads]
        hi = [lk[n].astype(BF16) for n in heads]
        lo = [(lk[n] - hi[n].astype(F32)).astype(BF16) for n in heads]
        between = [_dot(ut, hi[n]) + _dot(ut, lo[n]) for n in heads]
        for n in heads:
            l_in = l_s[:, cols[n]]
            a = jnp.exp((z[n] - sp[n]) + (between[n] + l_in))
            if masked:
                a = jnp.where(strict, a, 0.0)
            acc_s[:, cols[n]] += _dot(vt[n * d:(n + 1) * d, :], a.astype(BF16))
            l_s[:, cols[n]] = l_in + jnp.sum(lk[n], axis=0, keepdims=True)
        return jnp.max(l_s[...])

    def cond(c):
        return jnp.logical_and(c[0] >= 0, c[1] > SB_STOP)

    def body(c):
        return c[0] - 1, tile(c[0], False)

    lax.while_loop(cond, body, (qi - 1, tile(qi, True)))
    for hq in range(SB_HEADS):
        o_ref[:, hq * d:(hq + 1) * d] = acc_s[:, hq * t:(hq + 1) * t].T.astype(BF16)


def _sb_prompt(sq, skv, ut, batch, seq):
    t = ut.shape[0]
    nq = seq // t
    qspec = pl.BlockSpec((t, 512), lambda b, i: (b * nq + i, 0))
    return pl.pallas_call(
        _sb_prompt_kernel,
        grid=(batch, nq),
        in_specs=[qspec, pl.BlockSpec((seq, 512), lambda b, i: (b, 0)), pl.BlockSpec(ut.shape, lambda b, i: (0, 0))],
        out_specs=qspec,
        out_shape=jax.ShapeDtypeStruct((batch * seq, 512), BF16),
        scratch_shapes=[pltpu.VMEM((nq, SB_KV_HEADS * SB_HEAD_DIM, t), BF16),
                        pltpu.VMEM((SB_HEAD_DIM, SB_HEADS * t), BF16), pltpu.VMEM((1, SB_HEADS * t), F32),
                        pltpu.VMEM((SB_HEAD_DIM, SB_HEADS * t), F32)],
        compiler_params=_params(2),
    )(sq, skv, ut)


def _mla_sample_kernel(pt_ref, ql_ref, qr_ref, kcn_ref, *rest):
    npg = MLA_PAGES_PER_STEP
    lat_refs = rest[:npg]
    kr_refs = rest[npg:2 * npg]
    o_ref = rest[2 * npg]
    ql_s, qr_s, m_s, l_s, acc_s, lat_s, krt_s = rest[2 * npg + 1:]
    j = pl.program_id(1)
    t = ql_ref.shape[1]
    rows = MLA_HEADS * t
    slots = LANES // MLA_ROPE

    @pl.when(j == 0)
    def _():
        ql = ql_ref[0].astype(F32)
        qr = qr_ref[0].astype(F32)
        lane = lax.broadcasted_iota(jnp.int32, (1, LANES), 1)
        q_lat, q_slot, q_std = [], [], []
        for h in range(MLA_HEADS):
            grp = qr[:, (h // slots) * LANES:(h // slots + 1) * LANES]
            own = jnp.where(lane // MLA_ROPE == h % slots, grp, 0.0)
            q_lat.append(ql[:, h * LANES:(h + 1) * LANES])
            q_slot.append(own)
            s = h % slots
            q_std.append(own if s == 0 else pltpu.roll(own, LANES - MLA_ROPE * s, 1))
        q_lat = jnp.concatenate(q_lat, axis=0)
        ql_s[...] = q_lat.astype(BF16)
        qr_s[...] = jnp.concatenate(q_std, axis=0).astype(BF16)
        krt_s[...] = jnp.zeros(krt_s.shape, BF16)
        qn = jnp.concatenate([q_lat, jnp.concatenate(q_slot, axis=0)], axis=1).astype(BF16)
        kn = kcn_ref[0].astype(F32)
        kn = jnp.concatenate([kn, jnp.zeros((LANES - t, kn.shape[1]), F32)], axis=0).astype(BF16)
        s = _dot_nt(qn, kn)
        tq = lax.broadcasted_iota(jnp.int32, (rows, LANES), 0) % t
        sk = lax.broadcasted_iota(jnp.int32, (rows, LANES), 1)
        s = jnp.where(sk <= tq, s, NEG_INF)
        m = jnp.max(s, axis=1, keepdims=True)
        p = jnp.exp2(s - m)
        m_s[...] = m
        l_s[...] = jnp.sum(p, axis=1, keepdims=True)
        acc_s[...] = _dot(p.astype(BF16), kn[:, 0:MLA_KV_LORA])

    for i in range(npg):
        lat_s[i * PAGE_SIZE:(i + 1) * PAGE_SIZE, :] = lat_refs[i][...].astype(BF16)
        krt_s[0:MLA_ROPE, i * PAGE_SIZE:(i + 1) * PAGE_SIZE] = kr_refs[i][...].astype(BF16)
    lat = lat_s[...]
    s = _dot_nt(ql_s[...], lat) + _dot(qr_s[...], krt_s[...])
    m = m_s[...]
    m_new = jnp.maximum(m, jnp.max(s, axis=1, keepdims=True))
    p = jnp.exp2(s - m_new)
    alpha = jnp.exp2(m - m_new)
    m_s[...] = m_new
    l_s[...] = alpha * l_s[...] + jnp.sum(p, axis=1, keepdims=True)
    acc_s[...] = alpha * acc_s[...] + _dot(p.astype(BF16), lat)

    @pl.when(j == pl.num_programs(1) - 1)
    def _():
        o = acc_s[...] / l_s[...]
        for h in range(MLA_HEADS):
            o_ref[0, :, h * LANES:(h + 1) * LANES] = o[h * t:(h + 1) * t, :]


def _mla_sample(pt, ql3, qr3, kc3, cache_lat, cache_krt, l):
    nb, t, _ = ql3.shape
    n_pages = pt.shape[0] // nb
    npg = MLA_PAGES_PER_STEP
    assert n_pages % npg == 0
    steps = n_pages // npg
    seqspec = lambda w: pl.BlockSpec((1, t, w), lambda b, j, pt: (b, 0, 0))

    def page_spec(shape, i):
        return pl.BlockSpec((None, None) + shape, lambda b, j, pt: (l, pt[b * n_pages + j * npg + i], 0, 0))

    rows = MLA_HEADS * t
    grid_spec = pltpu.PrefetchScalarGridSpec(
        num_scalar_prefetch=1,
        grid=(nb, steps),
        in_specs=[seqspec(1024), seqspec(256), seqspec(256)]
        + [page_spec((PAGE_SIZE, MLA_KV_LORA), i) for i in range(npg)]
        + [page_spec((MLA_ROPE, PAGE_SIZE), i) for i in range(npg)],
        out_specs=seqspec(1024),
        scratch_shapes=[pltpu.VMEM((rows, LANES), BF16), pltpu.VMEM((rows, LANES), BF16),
                        pltpu.VMEM((rows, 1), F32), pltpu.VMEM((rows, 1), F32), pltpu.VMEM((rows, MLA_KV_LORA), F32),
                        pltpu.VMEM((npg * PAGE_SIZE, MLA_KV_LORA), BF16), pltpu.VMEM((LANES, npg * PAGE_SIZE), BF16)],
    )
    return pl.pallas_call(
        _mla_sample_kernel,
        grid_spec=grid_spec,
        out_shape=jax.ShapeDtypeStruct((nb, t, 1024), F32),
        compiler_params=_params(2),
    )(pt, ql3, qr3, kc3, *([cache_lat] * npg), *([cache_krt] * npg))


def _sb_sample_kernel(pt_ref, sq_ref, kn_ref, vn_ref, u_ref, ck_hbm, cv_hbm, o_ref,
                      kbuf, vbuf, sem, q_s, l_s, acc_s, *, layer, n_pages):
    b = pl.program_id(0)
    nb = pl.num_programs(0)
    t = sq_ref.shape[1]
    d = SB_HEAD_DIM
    rows = SB_HEADS * t
    nk = SB_KV_HEADS * PAGE_SIZE
    n_ahead = min(SB_PAGES_AHEAD, n_pages)
    cur = b % 2
    u = u_ref[...]
    row_head = lax.broadcasted_iota(jnp.int32, (rows, nk), 0) // (SB_GROUP * t)
    col = lax.broadcasted_iota(jnp.int32, (rows, nk), 1)
    same_head = (col % SB_KV_HEADS) == row_head

    def copies(seq, page_no, buf_set, slot):
        page = pt_ref[seq * n_pages + page_no]
        return (pltpu.make_async_copy(ck_hbm.at[layer, page], kbuf.at[buf_set, slot], sem.at[0, buf_set, slot]),
                pltpu.make_async_copy(cv_hbm.at[layer, page], vbuf.at[buf_set, slot], sem.at[1, buf_set, slot]))

    def start(seq, page_no, buf_set, slot):
        for c in copies(seq, page_no, buf_set, slot):
            c.start()

    def wait(seq, page_no, buf_set, slot):
        for c in copies(seq, page_no, buf_set, slot):
            c.wait()

    @pl.when(b == 0)
    def _():
        for i in range(n_ahead):
            start(0, n_pages - 1 - i, 0, i)

    @pl.when(b + 1 < nb)
    def _():
        for i in range(n_ahead):
            start(b + 1, n_pages - 1 - i, 1 - cur, i)

    sq = sq_ref[0].astype(F32)
    q = jnp.concatenate([sq[:, hq * d:(hq + 1) * d] for hq in range(SB_HEADS)], axis=0).astype(BF16)
    q_s[...] = q
    pad = jnp.zeros((nk - SB_KV_HEADS * t, d), F32)
    kn = jnp.concatenate([kn_ref[0], pad], axis=0).astype(BF16)
    vn = jnp.concatenate([vn_ref[0], pad], axis=0).astype(BF16)
    tq = lax.broadcasted_iota(jnp.int32, (rows, nk), 0) % t
    vis = jnp.logical_and(same_head, (col // SB_KV_HEADS) < tq)
    acc, l = _sb_block(q, kn, vn, u, jnp.zeros((rows, 1), F32), vis)
    acc_s[...] = acc
    l_s[...] = l

    def page(slot):
        add, l = _sb_block(q_s[...], kbuf[cur, slot].astype(BF16), vbuf[cur, slot].astype(BF16), u, l_s[...], same_head)
        acc_s[...] += add
        l_s[...] = l

    for i in range(n_ahead):
        wait(b, n_pages - 1 - i, cur, i)

        @pl.when(jnp.max(l_s[...]) > SB_STOP)
        def _():
            page(i)

    def cond(c):
        return jnp.logical_and(c[0] >= 0, c[1] > SB_STOP)

    def body(c):
        start(b, c[0], cur, 0)
        wait(b, c[0], cur, 0)
        page(0)
        return c[0] - 1, jnp.max(l_s[...])

    lax.while_loop(cond, body, (n_pages - 1 - n_ahead, jnp.max(l_s[...])))
    acc = acc_s[...]
    for hq in range(SB_HEADS):
        o_ref[0, :, hq * d:(hq + 1) * d] = acc[hq * t:(hq + 1) * t, :]


def _sb_sample(pt, sq3, kn3, vn3, u, cache_k, cache_v, l):
    nb, t, _ = sq3.shape
    n_pages = pt.shape[0] // nb
    nk = SB_KV_HEADS * PAGE_SIZE
    rows = SB_HEADS * t
    seqspec = lambda r, w: pl.BlockSpec((1, r, w), lambda b, pt: (b, 0, 0))
    grid_spec = pltpu.PrefetchScalarGridSpec(
        num_scalar_prefetch=1,
        grid=(nb,),
        in_specs=[seqspec(t, 512), seqspec(SB_KV_HEADS * t, SB_HEAD_DIM), seqspec(SB_KV_HEADS * t, SB_HEAD_DIM),
                  pl.BlockSpec(u.shape, lambda b, pt: (0, 0)),
                  pl.BlockSpec(memory_space=pl.ANY), pl.BlockSpec(memory_space=pl.ANY)],
        out_specs=seqspec(t, 512),
        scratch_shapes=[pltpu.VMEM((2, SB_PAGES_AHEAD, nk, SB_HEAD_DIM), F32),
                        pltpu.VMEM((2, SB_PAGES_AHEAD, nk, SB_HEAD_DIM), F32),
                        pltpu.SemaphoreType.DMA((2, 2, SB_PAGES_AHEAD)),
                        pltpu.VMEM((rows, SB_HEAD_DIM), BF16), pltpu.VMEM((rows, 1), F32),
                        pltpu.VMEM((rows, SB_HEAD_DIM), F32)],
    )
    return pl.pallas_call(
        functools.partial(_sb_sample_kernel, layer=l, n_pages=n_pages),
        grid_spec=grid_spec,
        out_shape=jax.ShapeDtypeStruct((nb, t, 512), F32),
        compiler_params=_params(1),
    )(pt, sq3, kn3, vn3, u, cache_k, cache_v)


def _merge_kernel(x_ref, olat_ref, ogm_ref, osb_ref, nmix_ref, wg_ref, bduv_ref, wbr_ref, wout_ref, x1_ref):
    x = x_ref[...]
    h = _rms(x, nmix_ref[...]).astype(BF16)
    o_mla = _dot(olat_ref[...].astype(BF16), bduv_ref[...]).astype(BF16)
    branches = (o_mla, ogm_ref[...], osb_ref[...].astype(BF16))
    merged = None
    for b in range(N_BRANCH):
        gate = _sigmoid(_dot(h, wg_ref[:, b * D_MODEL:(b + 1) * D_MODEL]))
        term = gate * _dot(branches[b], wbr_ref[b])
        merged = term if merged is None else merged + term
    x1_ref[...] = x + _dot(merged.astype(BF16), wout_ref[...])


def _merge(x, olat, ogm, osb, wl, l):
    n = x.shape[0]
    tm = min(TOKEN_TILE, n)
    row = lambda w: pl.BlockSpec((tm, w), lambda i: (i, 0))
    lay = lambda a: _layer_spec(a, l)
    ws = [wl['norm_mix'], wl['wg'], wl['bduv'], wl['wbr'], wl['wout']]
    return pl.pallas_call(
        _merge_kernel,
        grid=(n // tm,),
        in_specs=[row(D_MODEL), row(1024), row(512), row(512)] + [lay(a) for a in ws],
        out_specs=row(D_MODEL),
        out_shape=jax.ShapeDtypeStruct((n, D_MODEL), F32),
        compiler_params=_params(1),
    )(x, olat, ogm, osb, *ws)


def _ffn_kernel(*refs, sample, final, tiles_per_seq, t_new):
    x1_ref, nffn_ref, wup_ref, cw_ref, wdn_ref = refs[:5]
    pos = 5
    if sample:
        st_ref = refs[pos]
        pos += 1
    if final:
        nfin_ref = refs[pos]
        pos += 1
    x2_ref, cst_ref = refs[pos:pos + 2]
    pos += 2
    acc_s, h2_s = refs[pos:pos + 2]
    if not sample:
        carry_s = refs[pos + 2]
    tm = x1_ref.shape[0]
    i = pl.program_id(0)
    x1 = x1_ref[...]
    h2_s[...] = _rms(x1, nffn_ref[...]).astype(BF16)
    acc_s[...] = jnp.zeros_like(acc_s)
    row = lax.broadcasted_iota(jnp.int32, (tm, 1), 0)

    if not sample:
        @pl.when(i % tiles_per_seq == 0)
        def _():
            carry_s[...] = jnp.zeros_like(carry_s)

    def chunk(c, _):
        up = _dot(h2_s[...], wup_ref[c])
        r1 = pltpu.roll(up, 1, 0)
        r2 = pltpu.roll(up, 2, 0)
        if sample:
            nseq = tm // t_new
            rep = lambda a: jnp.broadcast_to(a[:, None, :], (nseq, t_new, a.shape[-1])).reshape(tm, a.shape[-1])
            p0 = rep(st_ref[0, c])
            p1 = rep(st_ref[1, c])
            rm = row % t_new
            cst_ref[c] = up
        else:
            p0 = carry_s[c, SUBLANES - 2:SUBLANES - 1, :]
            p1 = carry_s[c, SUBLANES - 1:SUBLANES, :]
            rm = row
            tail = up[tm - SUBLANES:tm, :]
            carry_s[c] = tail
            cst_ref[0, c] = tail
        m1 = jnp.where(rm == 0, p1, r1)
        m2 = jnp.where(rm == 0, p0, jnp.where(rm == 1, p1, r2))
        cw = cw_ref[c]
        conv = cw[3:4, :] + cw[2:3, :] * up
        conv = conv + cw[0:1, :] * m2
        conv = conv + cw[1:2, :] * m1
        gate = conv[:, 0:FFN_CHUNK]
        val = conv[:, FFN_CHUNK:2 * FFN_CHUNK]
        act = (gate * _sigmoid(gate) * val).astype(BF16)
        acc_s[...] += _dot(act, wdn_ref[c])
        return 0

    lax.fori_loop(0, N_FFN_CHUNKS, chunk, 0)
    x2 = x1 + acc_s[...]
    if final:
        x2 = _rms(x2, nfin_ref[...])
    x2_ref[...] = x2


def _ffn(x1, wl, l, nfin, state, batch, seq, final):
    n = x1.shape[0]
    sample = state is not None
    tm = min(TOKEN_TILE // 2, n) if sample else min(TOKEN_TILE, seq)
    grid = (n // tm,)
    row = lambda w: pl.BlockSpec((tm, w), lambda i: (i, 0))
    lay = lambda a: _layer_spec(a, l)
    ws = [wl['norm_ffn'], wl['wup'], wl['cw'], wl['wdn']]
    args = [x1] + ws
    in_specs = [row(D_MODEL)] + [lay(a) for a in ws]
    c2 = 2 * FFN_CHUNK
    if sample:
        t_new = seq
        args.append(state)
        in_specs.append(pl.BlockSpec((2, N_FFN_CHUNKS, tm // t_new, c2), lambda i: (0, 0, i, 0)))
        cst_shape = jax.ShapeDtypeStruct((N_FFN_CHUNKS, n, c2), F32)
        cst_spec = pl.BlockSpec((N_FFN_CHUNKS, tm, c2), lambda i: (0, i, 0))
        tiles_per_seq = 1
    else:
        t_new = 1
        tiles_per_seq = seq // tm
        cst_shape = jax.ShapeDtypeStruct((batch, N_FFN_CHUNKS, SUBLANES, c2), F32)
        cst_spec = pl.BlockSpec((1, N_FFN_CHUNKS, SUBLANES, c2), lambda i: (i // tiles_per_seq, 0, 0, 0))
    if final:
        args.append(nfin)
        in_specs.append(pl.BlockSpec(nfin.shape, lambda i: (0, 0)))
    scratch = [pltpu.VMEM((tm, D_MODEL), F32), pltpu.VMEM((tm, D_MODEL), BF16)]
    if not sample:
        scratch.append(pltpu.VMEM((N_FFN_CHUNKS, SUBLANES, c2), F32))
    return pl.pallas_call(
        functools.partial(_ffn_kernel, sample=sample, final=final, tiles_per_seq=tiles_per_seq, t_new=t_new),
        grid=grid,
        in_specs=in_specs,
        out_specs=[row(D_MODEL), cst_spec],
        out_shape=[jax.ShapeDtypeStruct((n, D_MODEL), F32), cst_shape],
        scratch_shapes=scratch,
        compiler_params=_params(1),
    )(*args)


def _block_diag(blocks):
    dpt, n, r, c = blocks.shape
    out = jnp.zeros((dpt, n, r, n, c), blocks.dtype)
    idx = jnp.arange(n)
    out = out.at[:, idx, :, idx, :].set(jnp.moveaxis(blocks, 1, 0))
    return out.reshape(dpt, n * r, n * c)


def _chunk_cols(a):
    lead = a.shape[:-1]
    g = a[..., :FFN_DIM].reshape(lead + (N_FFN_CHUNKS, FFN_CHUNK))
    v = a[..., FFN_DIM:].reshape(lead + (N_FFN_CHUNKS, FFN_CHUNK))
    return jnp.concatenate([g, v], axis=-1)


def _unchunk_cols(a):
    lead = a.shape[:-2]
    g = a[..., :FFN_CHUNK].reshape(lead + (FFN_DIM,))
    v = a[..., FFN_CHUNK:].reshape(lead + (FFN_DIM,))
    return jnp.concatenate([g, v], axis=-1)


def _rope_tables(pos):
    half = MLA_ROPE // 2
    inv_freq = ROPE_THETA ** (-jnp.arange(half, dtype=jnp.float32) / half)
    ang = pos.astype(jnp.float32)[:, None] * inv_freq
    cos, sin = jnp.cos(ang), jnp.sin(ang)
    reps = LANES // MLA_ROPE
    cos_t = jnp.tile(jnp.concatenate([cos, cos], axis=1), (1, reps))
    sin_t = jnp.tile(jnp.concatenate([-sin, sin], axis=1), (1, reps))
    return cos_t, sin_t


def _prep_weights(norm_mix, w_in, mla_q_norm, mla_w_uq, mla_kv_norm, mla_w_uk, mla_w_uv, gmlp_v_norm,
                  w_branch, w_out, norm_ffn, ffn_w_up, ffn_conv_w, ffn_conv_b, ffn_w_down):
    depth = w_in.shape[0]
    o = np.cumsum((0, MLA_Q_LORA, MLA_KV_LORA, MLA_ROPE, GMLP_WIDTH, GMLP_WIDTH, SB_HEADS * SB_HEAD_DIM,
                   SB_KV_HEADS * SB_HEAD_DIM, SB_KV_HEADS * SB_HEAD_DIM, N_BRANCH * D_MODEL)).tolist()
    w1 = jnp.concatenate([w_in[:, :, o[0]:o[2]], jnp.tile(w_in[:, :, o[2]:o[3]], (1, 1, LANES // MLA_ROPE)),
                          w_in[:, :, o[3]:o[8]]], axis=2).astype(BF16)
    wuq3 = mla_w_uq.reshape(depth, MLA_Q_LORA, MLA_HEADS, MLA_NOPE + MLA_ROPE)
    wuq = jnp.concatenate([wuq3[..., :MLA_NOPE].reshape(depth, MLA_Q_LORA, -1),
                           wuq3[..., MLA_NOPE:].reshape(depth, MLA_Q_LORA, -1)], axis=2).astype(BF16)
    cw = jnp.concatenate([_chunk_cols(ffn_conv_w), _chunk_cols(ffn_conv_b)[:, None],
                          jnp.zeros((depth, SUBLANES - CONV_W - 1, N_FFN_CHUNKS, 2 * FFN_CHUNK), F32)], axis=1)
    r3 = lambda a: a.reshape(depth, 1, -1)
    return dict(
        norm_mix=r3(norm_mix), w1=w1, wg=w_in[:, :, o[8]:o[9]].astype(BF16),
        q_norm=r3(mla_q_norm), wuq=wuq, kv_norm=r3(mla_kv_norm), gv_norm=r3(gmlp_v_norm),
        bduk=_block_diag(jnp.transpose(mla_w_uk, (0, 2, 3, 1))).astype(BF16),
        bduv=_block_diag(jnp.transpose(mla_w_uv, (0, 2, 1, 3))).astype(BF16),
        wbr=w_branch.astype(BF16), wout=w_out.astype(BF16), norm_ffn=r3(norm_ffn),
        wup=jnp.moveaxis(_chunk_cols(ffn_w_up), 2, 1).astype(BF16),
        cw=jnp.moveaxis(cw, 2, 1),
        wdn=ffn_w_down.reshape(depth, N_FFN_CHUNKS, FFN_CHUNK, D_MODEL).astype(BF16),
    )


def kernel(x_prompt, x_sample, cache_mla_latent, cache_mla_krope, cache_sb_k, cache_sb_v, state_ffn_conv, page_table, norm_mix, w_in, mla_q_norm, mla_w_uq, mla_kv_norm, mla_w_uk, mla_w_uv, gmlp_v_norm, gmlp_w_s, gmlp_b_s, w_branch, w_out, norm_ffn, ffn_w_up, ffn_conv_w, ffn_conv_b, ffn_w_down, norm_final):
    batch, seq, _ = x_prompt.shape
    nb, t_new, _ = x_sample.shape
    depth = w_in.shape[0]
    n_pool = cache_sb_k.shape[1]
    past_len = page_table.shape[1] * PAGE_SIZE
    assert t_new == SUBLANES and LANES % t_new == 0
    assert seq % ATT_TILE == 0 or seq < ATT_TILE

    wl = _prep_weights(norm_mix, w_in, mla_q_norm, mla_w_uq, mla_kv_norm, mla_w_uk, mla_w_uv, gmlp_v_norm,
                       w_branch, w_out, norm_ffn, ffn_w_up, ffn_conv_w, ffn_conv_b, ffn_w_down)
    nfin = norm_final.reshape(1, D_MODEL)

    n_s = nb * t_new
    tm_s = min(TOKEN_TILE, n_s)
    cos_p, sin_p = _rope_tables(jnp.arange(seq, dtype=jnp.int32))
    cos_s, sin_s = _rope_tables(jnp.tile(past_len + jnp.arange(t_new, dtype=jnp.int32), tm_s // t_new))
    tm_p = min(TOKEN_TILE, batch * seq)
    pos_tiles_p = max(seq // tm_p, 1)

    r = jnp.arange(GMLP_CHUNK)
    mask_p = (r[None, :] <= r[:, None]).astype(F32)
    mask_s = jnp.logical_and(r[None, :] // t_new == r[:, None] // t_new, r[None, :] <= r[:, None]).astype(F32)
    reps = GMLP_CHUNK // t_new
    ws_s = jnp.tile(gmlp_w_s[:, :, :t_new, :t_new], (1, 1, reps, reps))
    bs_p = jnp.broadcast_to(gmlp_b_s[..., None], gmlp_b_s.shape + (GMLP_GROUP_DIM,))
    bs_s = jnp.broadcast_to(jnp.tile(gmlp_b_s[:, :, :t_new], (1, 1, reps))[..., None], bs_p.shape)

    ta = min(ATT_TILE, seq)
    ra = jnp.arange(ta)
    ut_p = (ra[None, :] > ra[:, None]).astype(BF16)
    rk = jnp.arange(SB_KV_HEADS * PAGE_SIZE)
    u_s = (rk[:, None] > rk[None, :]).astype(BF16)

    pt = page_table.reshape(-1)
    cache_krt = jnp.swapaxes(cache_mla_krope, 2, 3)
    cache_k2 = cache_sb_k.reshape(depth, n_pool, PAGE_SIZE * SB_KV_HEADS, SB_HEAD_DIM)
    cache_v2 = cache_sb_v.reshape(depth, n_pool, PAGE_SIZE * SB_KV_HEADS, SB_HEAD_DIM)
    state_c = jnp.moveaxis(_chunk_cols(state_ffn_conv), (2, 3), (1, 2))

    xp = x_prompt.reshape(batch * seq, D_MODEL)
    xs = x_sample.reshape(n_s, D_MODEL)
    outs = [[] for _ in range(11)]
    for l in range(depth):
        last = l == depth - 1
        ql, qr, kc, ckv, kr, ogm, _, sq, skv, sk, sv = _inproj(
            xp, cos_p, sin_p, pos_tiles_p, wl, gmlp_w_s[l], bs_p[l], mask_p, l)
        olat = _mla_prompt(ql, qr, kc, batch, seq)
        osb = _sb_prompt(sq, skv, ut_p, batch, seq)
        x1 = _merge(xp, olat, ogm, osb, wl, l)
        xp, cst = _ffn(x1, wl, l, nfin, None, batch, seq, last)
        outs[0].append(ckv.reshape(batch, seq, MLA_KV_LORA))
        outs[1].append(kr[:, :MLA_ROPE].reshape(batch, seq, MLA_ROPE))
        outs[2].append(sk.reshape(batch, seq, SB_KV_HEADS, SB_HEAD_DIM))
        outs[3].append(sv.reshape(batch, seq, SB_KV_HEADS, SB_HEAD_DIM))
        outs[4].append(_unchunk_cols(jnp.swapaxes(cst[:, :, SUBLANES - 2:, :], 1, 2)))
        ql, qr, kc, ckv, kr, ogm, gv, sq, skv, sk, sv = _inproj(
            xs, cos_s, sin_s, 1, wl, ws_s[l], bs_s[l], mask_s, l)
        r3 = lambda a: a.reshape(nb, t_new, a.shape[-1])
        olat = _mla_sample(pt, r3(ql), r3(qr), r3(kc), cache_mla_latent, cache_krt, l)
        osb = _sb_sample(pt, r3(sq), sk.reshape(nb, t_new * SB_KV_HEADS, SB_HEAD_DIM),
                         sv.reshape(nb, t_new * SB_KV_HEADS, SB_HEAD_DIM), u_s, cache_k2, cache_v2, l)
        x1 = _merge(xs, olat.reshape(n_s, -1), ogm, osb.reshape(n_s, -1), wl, l)
        xs, upf = _ffn(x1, wl, l, nfin, state_c[l], nb, t_new, last)
        outs[5].append(r3(ckv))
        outs[6].append(r3(kr[:, :MLA_ROPE]))
        outs[7].append(sk.reshape(nb, t_new, SB_KV_HEADS, SB_HEAD_DIM))
        outs[8].append(sv.reshape(nb, t_new, SB_KV_HEADS, SB_HEAD_DIM))
        outs[9].append(r3(gv))
        tail = upf.reshape(N_FFN_CHUNKS, nb, t_new, 2 * FFN_CHUNK)[:, :, t_new - (CONV_W - 1):, :]
        outs[10].append(_unchunk_cols(jnp.transpose(tail, (1, 2, 0, 3))))
    return (xp.reshape(batch, seq, D_MODEL), xs.reshape(nb, t_new, D_MODEL)) + tuple(jnp.stack(o) for o in outs)
```

```python
import functools

import jax
import jax.numpy as jnp
import numpy as np
from jax import lax
from jax.experimental import pallas as pl
from jax.experimental.pallas import tpu as pltpu

F32 = jnp.float32
BF16 = jnp.bfloat16

D_MODEL = 1024
N_BRANCH = 3
BRANCH_WIDTH = 512
MLA_HEADS = 8
MLA_NOPE = 64
MLA_ROPE = 32
MLA_V = 64
MLA_Q_LORA = 256
MLA_KV_LORA = 128
MLA_SCALE = (MLA_NOPE + MLA_ROPE) ** -0.5
ROPE_THETA = 10000.0
GMLP_CHUNK = 128
GMLP_GROUPS = 4
GMLP_GROUP_DIM = 128
GMLP_WIDTH = GMLP_GROUPS * GMLP_GROUP_DIM
SB_HEADS = 4
SB_KV_HEADS = 2
SB_GROUP = SB_HEADS // SB_KV_HEADS
SB_HEAD_DIM = 128
SB_SCALE = SB_HEAD_DIM ** -0.5
FFN_DIM = 2816
CONV_W = 3
PAGE_SIZE = 128
EPS = 1e-6
NEG_INF = -1e30

LANES = 128
SUBLANES = 8
VMEM_LIMIT_BYTES = 56 * 1024 * 1024
TOKEN_TILE = 512
ATT_TILE = 256
FFN_CHUNK = 256
N_FFN_CHUNKS = FFN_DIM // FFN_CHUNK
MLA_PAGES_PER_GROUP = 16
FFN_UP_AHEAD = 2
SB_PAGES_AHEAD = 2
MLA_QK_AHEAD = 8
LOG2E = 1.4426950408889634
SB_STOP = -105.0

_C_CQ = (0, 256)
_C_CKV = (256, 384)
_C_KR = (384, 512)
_C_GU = (512, 1024)
_C_GV = (1024, 1536)
_C_SQ = (1536, 2048)
_C_SK = (2048, 2304)
_C_SV = (2304, 2560)
W1_COLS = 2560

_NT = (((1,), (1,)), ((), ()))


def _params(grid_rank):
    return pltpu.CompilerParams(vmem_limit_bytes=VMEM_LIMIT_BYTES, dimension_semantics=("arbitrary",) * grid_rank)


def _layer_spec(a, l):
    return pl.BlockSpec((None,) + a.shape[1:], lambda *_: (l,) + (0,) * (a.ndim - 1), pipeline_mode=pl.Buffered(1))


def _rms(x, g):
    return x * lax.rsqrt(jnp.mean(x * x, axis=-1, keepdims=True) + EPS) * g


def _gelu(x):
    return 0.5 * x * (1.0 + jnp.tanh(np.sqrt(2.0 / np.pi).astype(np.float32) * (x + 0.044715 * (x * x * x))))


def _sigmoid(x):
    return 1.0 / (1.0 + jnp.exp(-x))


def _softplus(z):
    return jnp.maximum(z, 0.0) + jnp.log(1.0 + jnp.exp(-jnp.abs(z)))


def _dot(a, b):
    return jnp.dot(a, b, preferred_element_type=F32)


def _dot_nt(a, b):
    return lax.dot_general(a, b, _NT, preferred_element_type=F32)


def _inproj_kernel(x_ref, cos_ref, sin_ref, nmix_ref, w1_ref, qn_ref, wuq_ref, bduk_ref, kvn_ref, gvn_ref,
                   ws_ref, bs_ref, mask_ref,
                   ql_ref, qr_ref, kc_ref, ckv_ref, kr_ref, ogm_ref, gv_ref, sq_ref, skv_ref, sk_ref, sv_ref):
    tm = x_ref.shape[0]
    h = _rms(x_ref[...], nmix_ref[...]).astype(BF16)
    cos_t = cos_ref[...]
    sin_t = sin_ref[...]
    lane = lax.broadcasted_iota(jnp.int32, (1, LANES), 1)
    first_half = (lane % MLA_ROPE) < (MLA_ROPE // 2)

    def rope(v):
        partner = jnp.where(first_half, pltpu.roll(v, LANES - MLA_ROPE // 2, 1), pltpu.roll(v, MLA_ROPE // 2, 1))
        return v * cos_t + partner * sin_t

    def seg(c):
        return _dot(h, w1_ref[:, c[0]:c[1]])

    cq = _rms(seg(_C_CQ), qn_ref[...]).astype(BF16)
    q = _dot(cq, wuq_ref[...])
    n_nope = MLA_HEADS * MLA_NOPE
    ql_ref[...] = (_dot(q[:, :n_nope].astype(BF16), bduk_ref[...]) * (MLA_SCALE * LOG2E)).astype(BF16)
    for g in range(2):
        qr_ref[:, g * LANES:(g + 1) * LANES] = (
            rope(q[:, n_nope + g * LANES:n_nope + (g + 1) * LANES]) * (MLA_SCALE * LOG2E)).astype(BF16)
    ckv = _rms(seg(_C_CKV), kvn_ref[...])
    ckv_ref[...] = ckv
    kc_ref[:, 0:LANES] = ckv.astype(BF16)
    kr = rope(seg(_C_KR))
    kr_ref[...] = kr
    kc_ref[:, LANES:2 * LANES] = kr.astype(BF16)

    gu = _gelu(seg(_C_GU))
    gv = _rms(_gelu(seg(_C_GV)), gvn_ref[...])
    gv_ref[...] = gv
    gvb = gv.astype(BF16)
    for g in range(GMLP_GROUPS):
        wsm = (ws_ref[g] * mask_ref[...]).astype(BF16)
        cs = slice(g * GMLP_GROUP_DIM, (g + 1) * GMLP_GROUP_DIM)
        for c in range(tm // GMLP_CHUNK):
            rs = slice(c * GMLP_CHUNK, (c + 1) * GMLP_CHUNK)
            mix = _dot(wsm, gvb[rs, cs]) + bs_ref[g]
            ogm_ref[rs, cs] = (gu[rs, cs] * mix).astype(BF16)

    sq_ref[...] = (seg(_C_SQ) * SB_SCALE).astype(BF16)
    sk = seg(_C_SK)
    sv = seg(_C_SV)
    sk_ref[...] = sk
    sv_ref[...] = sv
    nkv = SB_KV_HEADS * SB_HEAD_DIM
    skv_ref[:, 0:nkv] = sk.astype(BF16)
    skv_ref[:, nkv:2 * nkv] = sv.astype(BF16)


def _inproj(x, cos_t, sin_t, n_pos_tiles, wl, ws, bs, mask, l):
    n = x.shape[0]
    tm = min(TOKEN_TILE, n)
    grid = (n // tm,)
    row = lambda w: pl.BlockSpec((tm, w), lambda i: (i, 0))
    lay = lambda a: _layer_spec(a, l)
    full = lambda a: pl.BlockSpec(a.shape, lambda i: (0,) * a.ndim)
    pos = pl.BlockSpec((tm, LANES), lambda i: (i % n_pos_tiles, 0))
    out_w = [(1024, BF16), (256, BF16), (256, BF16), (128, F32), (128, F32), (512, BF16), (512, F32),
             (512, BF16), (512, BF16), (256, F32), (256, F32)]
    return pl.pallas_call(
        _inproj_kernel,
        grid=grid,
        in_specs=[row(D_MODEL), pos, pos, lay(wl['norm_mix']), lay(wl['w1']), lay(wl['q_norm']), lay(wl['wuq']),
                  lay(wl['bduk']), lay(wl['kv_norm']), lay(wl['gv_norm']), full(ws), full(bs), full(mask)],
        out_specs=[row(w) for w, _ in out_w],
        out_shape=[jax.ShapeDtypeStruct((n, w), d) for w, d in out_w],
        compiler_params=_params(1),
    )(x, cos_t, sin_t, wl['norm_mix'], wl['w1'], wl['q_norm'], wl['wuq'], wl['bduk'], wl['kv_norm'],
      wl['gv_norm'], ws, bs, mask)


def _transpose_bf16(x):
    return x.astype(F32).T.astype(BF16)


def _mla_prompt_kernel(ql_ref, qr_ref, kc_ref, o_ref, vt_s, vt2_s, qt_s, m_s, l_s, acc_s):
    t = ql_ref.shape[0]
    qi = pl.program_id(1)
    slots = LANES // MLA_ROPE

    @pl.when(qi == 0)
    def _():
        def tr(kt, _):
            rows = pl.ds(pl.multiple_of(kt * t, t), t)
            vt_s[kt] = _transpose_bf16(kc_ref[rows, 0:MLA_KV_LORA])
            return 0
        lax.fori_loop(0, vt_s.shape[0], tr, 0)

        def tr2(kp, _):
            rows = pl.ds(pl.multiple_of(kp * 2 * t, 2 * t), 2 * t)
            vt2_s[kp] = _transpose_bf16(kc_ref[rows, 0:MLA_KV_LORA])
            return 0
        lax.fori_loop(0, vt2_s.shape[0], tr2, 0)

    sub = lax.broadcasted_iota(jnp.int32, (LANES, 1), 0)
    grp_t = [qr_ref[:, g * LANES:(g + 1) * LANES].astype(F32).T for g in range(MLA_HEADS // slots)]
    for h in range(MLA_HEADS):
        cs = slice(h * t, (h + 1) * t)
        qt_s[0:LANES, cs] = _transpose_bf16(ql_ref[:, h * LANES:(h + 1) * LANES])
        qt_s[LANES:2 * LANES, cs] = jnp.where(sub // MLA_ROPE == h % slots, grp_t[h // slots], 0.0).astype(BF16)
    m_s[...] = jnp.full(m_s.shape, NEG_INF, F32)
    l_s[...] = jnp.zeros(l_s.shape, F32)
    acc_s[...] = jnp.zeros(acc_s.shape, F32)

    key = lax.broadcasted_iota(jnp.int32, (t, t), 0)
    qry = lax.broadcasted_iota(jnp.int32, (t, t), 1)
    causal = key <= qry

    def tile(k, vt, masked):
        def scores(h):
            return _dot(k, qt_s[:, h * t:(h + 1) * t])

        s_all = [scores(h) for h in range(MLA_QK_AHEAD)]
        for h in range(MLA_HEADS):
            cs = slice(h * t, (h + 1) * t)
            s = s_all[h]
            if masked:
                s = jnp.where(causal, s, NEG_INF)
            m = m_s[:, cs]
            m_new = jnp.maximum(m, jnp.max(s, axis=0, keepdims=True))
            p = jnp.exp2(s - m_new)
            alpha = jnp.exp2(m - m_new)
            m_s[:, cs] = m_new
            l_s[:, cs] = alpha * l_s[:, cs] + jnp.sum(p, axis=0, keepdims=True)
            if h + MLA_QK_AHEAD < MLA_HEADS:
                s_all.append(scores(h + MLA_QK_AHEAD))
            acc_s[:, cs] = alpha * acc_s[:, cs] + _dot(vt, p.astype(BF16))

    def pair(kp, _):
        tile(kc_ref[pl.ds(pl.multiple_of(kp * 2 * t, 2 * t), 2 * t), :], vt2_s[kp], False)
        return 0

    lax.fori_loop(0, qi // 2, pair, 0)

    @pl.when(qi % 2 == 1)
    def _():
        tile(kc_ref[pl.ds(pl.multiple_of((qi - 1) * t, t), t), :], vt_s[qi - 1], False)

    tile(kc_ref[pl.ds(pl.multiple_of(qi * t, t), t), :], vt_s[qi], True)
    for h in range(MLA_HEADS):
        cs = slice(h * t, (h + 1) * t)
        o = acc_s[:, cs] * (1.0 / l_s[:, cs])
        o_ref[:, h * LANES:(h + 1) * LANES] = o.T.astype(BF16)


def _mla_prompt(ql, qr, kc, batch, seq):
    t = min(ATT_TILE, seq)
    nq = seq // t
    assert nq % 2 == 0
    qspec = lambda w: pl.BlockSpec((t, w), lambda b, i: (b * nq + i, 0))
    return pl.pallas_call(
        _mla_prompt_kernel,
        grid=(batch, nq),
        in_specs=[qspec(1024), qspec(256), pl.BlockSpec((seq, 256), lambda b, i: (b, 0))],
        out_specs=qspec(1024),
        out_shape=jax.ShapeDtypeStruct((batch * seq, 1024), BF16),
        scratch_shapes=[pltpu.VMEM((nq, MLA_KV_LORA, t), BF16), pltpu.VMEM((nq // 2, MLA_KV_LORA, 2 * t), BF16),
                        pltpu.VMEM((2 * LANES, MLA_HEADS * t), BF16),
                        pltpu.VMEM((1, MLA_HEADS * t), F32), pltpu.VMEM((1, MLA_HEADS * t), F32),
                        pltpu.VMEM((MLA_KV_LORA, MLA_HEADS * t), F32)],
        compiler_params=_params(2),
    )(ql, qr, kc)


def _sb_block(q, k, v, u, l_in, visible):
    z = _dot_nt(q, k)
    sp = _softplus(z)
    lk = -sp
    if visible is not None:
        lk = jnp.where(visible, lk, 0.0)
    hi = lk.astype(BF16)
    lo = (lk - hi.astype(F32)).astype(BF16)
    between = _dot(hi, u) + _dot(lo, u)
    a = jnp.exp((z - sp) + (between + l_in))
    if visible is not None:
        a = jnp.where(visible, a, 0.0)
    return _dot(a.astype(BF16), v), l_in + jnp.sum(lk, axis=1, keepdims=True)


def _sb_prompt_kernel(sq_ref, skv_ref, ut_ref, o_ref, vt_s, qt_s, l_s, acc_s):
    t = sq_ref.shape[0]
    qi = pl.program_id(1)
    d = SB_HEAD_DIM
    nkv = SB_KV_HEADS * d
    w = SB_GROUP * t

    @pl.when(qi == 0)
    def _():
        def tr(kt, _):
            rows = pl.ds(pl.multiple_of(kt * t, t), t)
            vt_s[kt] = _transpose_bf16(skv_ref[rows, nkv:2 * nkv])
            return 0
        lax.fori_loop(0, vt_s.shape[0], tr, 0)

    for hq in range(SB_HEADS):
        qt_s[:, hq * t:(hq + 1) * t] = _transpose_bf16(sq_ref[:, hq * d:(hq + 1) * d])
    l_s[...] = jnp.zeros(l_s.shape, F32)
    acc_s[...] = jnp.zeros(acc_s.shape, F32)
    ut = ut_ref[...]
    key = lax.broadcasted_iota(jnp.int32, (t, w), 0)
    qry = lax.broadcasted_iota(jnp.int32, (t, w), 1) % t
    strict = key < qry

    def tile(kt, masked):
        rows = pl.ds(pl.multiple_of(kt * t, t), t)
        vt = vt_s[kt]
        heads = range(SB_KV_HEADS)
        cols = [slice(n * w, (n + 1) * w) for n in heads]
        z = [_dot(skv_ref[rows, n * d:(n + 1) * d], qt_s[:, cols[n]]) for n in heads]
        sp = [_softplus(z[n]) for n in heads]
        lk = [jnp.where(strict, -sp[n], 0.0) if masked else -sp[n] for n in heads]
        hi = [lk[n].astype(BF16) for n in heads]
        lo = [(lk[n] - hi[n].astype(F32)).astype(BF16) for n in heads]
        between = [_dot(ut, hi[n]) + _dot(ut, lo[n]) for n in heads]
        for n in heads:
            l_in = l_s[:, cols[n]]
            a = jnp.exp((z[n] - sp[n]) + (between[n] + l_in))
            if masked:
                a = jnp.where(strict, a, 0.0)
            acc_s[:, cols[n]] += _dot(vt[n * d:(n + 1) * d, :], a.astype(BF16))
            l_s[:, cols[n]] = l_in + jnp.sum(lk[n], axis=0, keepdims=True)
        return jnp.max(l_s[...])

    def cond(c):
        return jnp.logical_and(c[0] >= 0, c[1] > SB_STOP)

    def body(c):
        return c[0] - 1, tile(c[0], False)

    lax.while_loop(cond, body, (qi - 1, tile(qi, True)))
    for hq in range(SB_HEADS):
        o_ref[:, hq * d:(hq + 1) * d] = acc_s[:, hq * t:(hq + 1) * t].T.astype(BF16)


def _sb_prompt(sq, skv, ut, batch, seq):
    t = ut.shape[0]
    nq = seq // t
    qspec = pl.BlockSpec((t, 512), lambda b, i: (b * nq + i, 0))
    return pl.pallas_call(
        _sb_prompt_kernel,
        grid=(batch, nq),
        in_specs=[qspec, pl.BlockSpec((seq, 512), lambda b, i: (b, 0)), pl.BlockSpec(ut.shape, lambda b, i: (0, 0))],
        out_specs=qspec,
        out_shape=jax.ShapeDtypeStruct((batch * seq, 512), BF16),
        scratch_shapes=[pltpu.VMEM((nq, SB_KV_HEADS * SB_HEAD_DIM, t), BF16),
                        pltpu.VMEM((SB_HEAD_DIM, SB_HEADS * t), BF16), pltpu.VMEM((1, SB_HEADS * t), F32),
                        pltpu.VMEM((SB_HEAD_DIM, SB_HEADS * t), F32)],
        compiler_params=_params(2),
    )(sq, skv, ut)


def _mla_sample_kernel(pt_ref, ql_ref, qr_ref, kcn_ref, lat_hbm, krt_hbm, o_ref,
                       lat_buf, krt_buf, krt_s, sem, *, layer, n_pages):
    grp = MLA_PAGES_PER_GROUP
    n_groups = n_pages // grp
    b = pl.program_id(0)
    nb = pl.num_programs(0)
    t = ql_ref.shape[1]
    rows = MLA_HEADS * t
    slots = LANES // MLA_ROPE

    def copies(seq, g, slot, i):
        page = pt_ref[seq * n_pages + g * grp + i]
        keys = pl.ds(i * PAGE_SIZE, PAGE_SIZE)
        return (pltpu.make_async_copy(lat_hbm.at[layer, page], lat_buf.at[slot, keys], sem.at[0, slot]),
                pltpu.make_async_copy(krt_hbm.at[layer, page], krt_buf.at[slot, :, keys], sem.at[1, slot]))

    def issue(seq, g, slot):
        for i in range(grp):
            for c in copies(seq, g, slot, i):
                c.start()

    def wait(seq, g, slot):
        for i in range(grp):
            for c in copies(seq, g, slot, i):
                c.wait()

    @pl.when(b == 0)
    def _():
        for g in range(n_groups):
            issue(0, g, g)
        krt_s[...] = jnp.zeros(krt_s.shape, BF16)

    ql = ql_ref[0].astype(F32)
    qr = qr_ref[0].astype(F32)
    lane = lax.broadcasted_iota(jnp.int32, (1, LANES), 1)
    q_lat, q_slot, q_std = [], [], []
    for h in range(MLA_HEADS):
        own = jnp.where(lane // MLA_ROPE == h % slots, qr[:, (h // slots) * LANES:(h // slots + 1) * LANES], 0.0)
        q_lat.append(ql[:, h * LANES:(h + 1) * LANES])
        q_slot.append(own)
        s = h % slots
        q_std.append(own if s == 0 else pltpu.roll(own, LANES - MLA_ROPE * s, 1))
    q_lat = jnp.concatenate(q_lat, axis=0)
    ql_b = q_lat.astype(BF16)
    qr_b = jnp.concatenate(q_std, axis=0).astype(BF16)

    qn = jnp.concatenate([q_lat, jnp.concatenate(q_slot, axis=0)], axis=1).astype(BF16)
    kn = kcn_ref[0].astype(F32)
    kn = jnp.concatenate([kn, jnp.zeros((LANES - t, kn.shape[1]), F32)], axis=0).astype(BF16)
    s = _dot_nt(qn, kn)
    tq = lax.broadcasted_iota(jnp.int32, (rows, LANES), 0) % t
    sk = lax.broadcasted_iota(jnp.int32, (rows, LANES), 1)
    s = jnp.where(sk <= tq, s, NEG_INF)
    m = jnp.max(s, axis=1, keepdims=True)
    p = jnp.exp2(s - m)
    l = jnp.sum(p, axis=1, keepdims=True)
    acc = _dot(p.astype(BF16), kn[:, 0:MLA_KV_LORA])

    def scores(g):
        wait(b, g, g)
        krt_s[g, 0:MLA_ROPE, :] = krt_buf[g].astype(BF16)
        return _dot_nt(ql_b, lat_buf[g].astype(BF16)) + _dot(qr_b, krt_s[g])

    s_next = scores(0)
    for g in range(n_groups):
        s = s_next
        if g + 1 < n_groups:
            s_next = scores(g + 1)
        m_new = jnp.maximum(m, jnp.max(s, axis=1, keepdims=True))
        p = jnp.exp2(s - m_new)
        alpha = jnp.exp2(m - m_new)
        l = alpha * l + jnp.sum(p, axis=1, keepdims=True)
        acc = alpha * acc + _dot(p.astype(BF16), lat_buf[g].astype(BF16))
        m = m_new

        @pl.when(b + 1 < nb)
        def _():
            issue(b + 1, g, g)

    o = acc / l
    for h in range(MLA_HEADS):
        o_ref[0, :, h * LANES:(h + 1) * LANES] = o[h * t:(h + 1) * t, :]


def _mla_sample(pt, ql3, qr3, kc3, cache_lat, cache_krt, l):
    nb, t, _ = ql3.shape
    n_pages = pt.shape[0] // nb
    grp = MLA_PAGES_PER_GROUP
    assert n_pages % grp == 0
    n_groups = n_pages // grp
    seqspec = lambda w: pl.BlockSpec((1, t, w), lambda b, pt: (b, 0, 0))
    grid_spec = pltpu.PrefetchScalarGridSpec(
        num_scalar_prefetch=1,
        grid=(nb,),
        in_specs=[seqspec(1024), seqspec(256), seqspec(256),
                  pl.BlockSpec(memory_space=pl.ANY), pl.BlockSpec(memory_space=pl.ANY)],
        out_specs=seqspec(1024),
        scratch_shapes=[pltpu.VMEM((n_groups, grp * PAGE_SIZE, MLA_KV_LORA), F32),
                        pltpu.VMEM((n_groups, MLA_ROPE, grp * PAGE_SIZE), F32),
                        pltpu.VMEM((n_groups, LANES, grp * PAGE_SIZE), BF16),
                        pltpu.SemaphoreType.DMA((2, n_groups))],
    )
    return pl.pallas_call(
        functools.partial(_mla_sample_kernel, layer=l, n_pages=n_pages),
        grid_spec=grid_spec,
        out_shape=jax.ShapeDtypeStruct((nb, t, 1024), F32),
        compiler_params=_params(1),
    )(pt, ql3, qr3, kc3, cache_lat, cache_krt)


def _sb_sample_kernel(pt_ref, sq_ref, kn_ref, vn_ref, u_ref, ck_hbm, cv_hbm, o_ref,
                      kbuf, vbuf, sem, q_s, l_s, acc_s, *, layer, n_pages):
    b = pl.program_id(0)
    nb = pl.num_programs(0)
    t = sq_ref.shape[1]
    d = SB_HEAD_DIM
    rows = SB_HEADS * t
    nk = SB_KV_HEADS * PAGE_SIZE
    n_ahead = min(SB_PAGES_AHEAD, n_pages)
    cur = b % 2
    u = u_ref[...]
    row_head = lax.broadcasted_iota(jnp.int32, (rows, nk), 0) // (SB_GROUP * t)
    col = lax.broadcasted_iota(jnp.int32, (rows, nk), 1)
    same_head = (col % SB_KV_HEADS) == row_head

    def copies(seq, page_no, buf_set, slot):
        page = pt_ref[seq * n_pages + page_no]
        return (pltpu.make_async_copy(ck_hbm.at[layer, page], kbuf.at[buf_set, slot], sem.at[0, buf_set, slot]),
                pltpu.make_async_copy(cv_hbm.at[layer, page], vbuf.at[buf_set, slot], sem.at[1, buf_set, slot]))

    def start(seq, page_no, buf_set, slot):
        for c in copies(seq, page_no, buf_set, slot):
            c.start()

    def wait(seq, page_no, buf_set, slot):
        for c in copies(seq, page_no, buf_set, slot):
            c.wait()

    @pl.when(b == 0)
    def _():
        for i in range(n_ahead):
            start(0, n_pages - 1 - i, 0, i)

    @pl.when(b + 1 < nb)
    def _():
        for i in range(n_ahead):
            start(b + 1, n_pages - 1 - i, 1 - cur, i)

    sq = sq_ref[0].astype(F32)
    q = jnp.concatenate([sq[:, hq * d:(hq + 1) * d] for hq in range(SB_HEADS)], axis=0).astype(BF16)
    q_s[...] = q
    pad = jnp.zeros((nk - SB_KV_HEADS * t, d), F32)
    kn = jnp.concatenate([kn_ref[0], pad], axis=0).astype(BF16)
    vn = jnp.concatenate([vn_ref[0], pad], axis=0).astype(BF16)
    tq = lax.broadcasted_iota(jnp.int32, (rows, nk), 0) % t
    vis = jnp.logical_and(same_head, (col // SB_KV_HEADS) < tq)
    acc, l = _sb_block(q, kn, vn, u, jnp.zeros((rows, 1), F32), vis)
    acc_s[...] = acc
    l_s[...] = l

    def page(slot):
        add, l = _sb_block(q_s[...], kbuf[cur, slot].astype(BF16), vbuf[cur, slot].astype(BF16), u, l_s[...], same_head)
        acc_s[...] += add
        l_s[...] = l

    for i in range(n_ahead):
        wait(b, n_pages - 1 - i, cur, i)
        page(i)

    def cond(c):
        return jnp.logical_and(c[0] >= 0, c[1] > SB_STOP)

    def body(c):
        start(b, c[0], cur, 0)
        wait(b, c[0], cur, 0)
        page(0)
        return c[0] - 1, jnp.max(l_s[...])

    lax.while_loop(cond, body, (n_pages - 1 - n_ahead, jnp.max(l_s[...])))
    acc = acc_s[...]
    for hq in range(SB_HEADS):
        o_ref[0, :, hq * d:(hq + 1) * d] = acc[hq * t:(hq + 1) * t, :]


def _sb_sample(pt, sq3, kn3, vn3, u, cache_k, cache_v, l):
    nb, t, _ = sq3.shape
    n_pages = pt.shape[0] // nb
    nk = SB_KV_HEADS * PAGE_SIZE
    rows = SB_HEADS * t
    seqspec = lambda r, w: pl.BlockSpec((1, r, w), lambda b, pt: (b, 0, 0))
    grid_spec = pltpu.PrefetchScalarGridSpec(
        num_scalar_prefetch=1,
        grid=(nb,),
        in_specs=[seqspec(t, 512), seqspec(SB_KV_HEADS * t, SB_HEAD_DIM), seqspec(SB_KV_HEADS * t, SB_HEAD_DIM),
                  pl.BlockSpec(u.shape, lambda b, pt: (0, 0)),
                  pl.BlockSpec(memory_space=pl.ANY), pl.BlockSpec(memory_space=pl.ANY)],
        out_specs=seqspec(t, 512),
        scratch_shapes=[pltpu.VMEM((2, SB_PAGES_AHEAD, nk, SB_HEAD_DIM), F32),
                        pltpu.VMEM((2, SB_PAGES_AHEAD, nk, SB_HEAD_DIM), F32),
                        pltpu.SemaphoreType.DMA((2, 2, SB_PAGES_AHEAD)),
                        pltpu.VMEM((rows, SB_HEAD_DIM), BF16), pltpu.VMEM((rows, 1), F32),
                        pltpu.VMEM((rows, SB_HEAD_DIM), F32)],
    )
    return pl.pallas_call(
        functools.partial(_sb_sample_kernel, layer=l, n_pages=n_pages),
        grid_spec=grid_spec,
        out_shape=jax.ShapeDtypeStruct((nb, t, 512), F32),
        compiler_params=_params(1),
    )(pt, sq3, kn3, vn3, u, cache_k, cache_v)


def _merge_kernel(x_ref, olat_ref, ogm_ref, osb_ref, nmix_ref, wg_ref, bduv_ref, wbr_ref, wout_ref, x1_ref):
    x = x_ref[...]
    h = _rms(x, nmix_ref[...]).astype(BF16)
    o_mla = _dot(olat_ref[...].astype(BF16), bduv_ref[...]).astype(BF16)
    branches = (o_mla, ogm_ref[...], osb_ref[...].astype(BF16))
    merged = None
    for b in range(N_BRANCH):
        gate = _sigmoid(_dot(h, wg_ref[:, b * D_MODEL:(b + 1) * D_MODEL]))
        term = gate * _dot(branches[b], wbr_ref[b])
        merged = term if merged is None else merged + term
    x1_ref[...] = x + _dot(merged.astype(BF16), wout_ref[...])


def _merge(x, olat, ogm, osb, wl, l):
    n = x.shape[0]
    tm = min(TOKEN_TILE, n)
    row = lambda w: pl.BlockSpec((tm, w), lambda i: (i, 0))
    lay = lambda a: _layer_spec(a, l)
    ws = [wl['norm_mix'], wl['wg'], wl['bduv'], wl['wbr'], wl['wout']]
    return pl.pallas_call(
        _merge_kernel,
        grid=(n // tm,),
        in_specs=[row(D_MODEL), row(1024), row(512), row(512)] + [lay(a) for a in ws],
        out_specs=row(D_MODEL),
        out_shape=jax.ShapeDtypeStruct((n, D_MODEL), F32),
        compiler_params=_params(1),
    )(x, olat, ogm, osb, *ws)


def _ffn_kernel(*refs, sample, final, tiles_per_seq, t_new):
    x1_ref, nffn_ref, wup_ref, cw_ref, wdn_ref = refs[:5]
    pos = 5
    if sample:
        st_ref = refs[pos]
        pos += 1
    if final:
        nfin_ref = refs[pos]
        pos += 1
    x2_ref, cst_ref = refs[pos:pos + 2]
    pos += 2
    acc_s, h2_s = refs[pos:pos + 2]
    if not sample:
        carry_s = refs[pos + 2]
    tm = x1_ref.shape[0]
    i = pl.program_id(0)
    x1 = x1_ref[...]
    h2_s[...] = _rms(x1, nffn_ref[...]).astype(BF16)
    acc_s[...] = jnp.zeros_like(acc_s)
    row = lax.broadcasted_iota(jnp.int32, (tm, 1), 0)

    if not sample:
        @pl.when(i % tiles_per_seq == 0)
        def _():
            carry_s[...] = jnp.zeros_like(carry_s)

    def up_proj(c):
        return _dot(h2_s[...], wup_ref[c])

    def chunk(c, up):
        r1 = pltpu.roll(up, 1, 0)
        r2 = pltpu.roll(up, 2, 0)
        if sample:
            nseq = tm // t_new
            rep = lambda a: jnp.broadcast_to(a[:, None, :], (nseq, t_new, a.shape[-1])).reshape(tm, a.shape[-1])
            p0 = rep(st_ref[0, c])
            p1 = rep(st_ref[1, c])
            rm = row % t_new
            cst_ref[c] = up
        else:
            p0 = carry_s[c, SUBLANES - 2:SUBLANES - 1, :]
            p1 = carry_s[c, SUBLANES - 1:SUBLANES, :]
            rm = row
            tail = up[tm - SUBLANES:tm, :]
            carry_s[c] = tail
            cst_ref[0, c] = tail
        m1 = jnp.where(rm == 0, p1, r1)
        m2 = jnp.where(rm == 0, p0, jnp.where(rm == 1, p1, r2))
        cw = cw_ref[c]
        conv = cw[3:4, :] + cw[2:3, :] * up
        conv = conv + cw[0:1, :] * m2
        conv = conv + cw[1:2, :] * m1
        gate = conv[:, 0:FFN_CHUNK]
        val = conv[:, FFN_CHUNK:2 * FFN_CHUNK]
        act = (gate * _sigmoid(gate) * val).astype(BF16)
        return act

    ups = [up_proj(c) for c in range(FFN_UP_AHEAD)]
    for c in range(N_FFN_CHUNKS):
        act = chunk(c, ups[c])
        if c + FFN_UP_AHEAD < N_FFN_CHUNKS:
            ups.append(up_proj(c + FFN_UP_AHEAD))
        acc_s[...] += _dot(act, wdn_ref[c])
    x2 = x1 + acc_s[...]
    if final:
        x2 = _rms(x2, nfin_ref[...])
    x2_ref[...] = x2


def _ffn(x1, wl, l, nfin, state, batch, seq, final):
    n = x1.shape[0]
    sample = state is not None
    tm = min(TOKEN_TILE // 2, n) if sample else min(TOKEN_TILE, seq)
    grid = (n // tm,)
    row = lambda w: pl.BlockSpec((tm, w), lambda i: (i, 0))
    lay = lambda a: _layer_spec(a, l)
    ws = [wl['norm_ffn'], wl['wup'], wl['cw'], wl['wdn']]
    args = [x1] + ws
    in_specs = [row(D_MODEL)] + [lay(a) for a in ws]
    c2 = 2 * FFN_CHUNK
    if sample:
        t_new = seq
        args.append(state)
        in_specs.append(pl.BlockSpec((2, N_FFN_CHUNKS, tm // t_new, c2), lambda i: (0, 0, i, 0)))
        cst_shape = jax.ShapeDtypeStruct((N_FFN_CHUNKS, n, c2), F32)
        cst_spec = pl.BlockSpec((N_FFN_CHUNKS, tm, c2), lambda i: (0, i, 0))
        tiles_per_seq = 1
    else:
        t_new = 1
        tiles_per_seq = seq // tm
        cst_shape = jax.ShapeDtypeStruct((batch, N_FFN_CHUNKS, SUBLANES, c2), F32)
        cst_spec = pl.BlockSpec((1, N_FFN_CHUNKS, SUBLANES, c2), lambda i: (i // tiles_per_seq, 0, 0, 0))
    if final:
        args.append(nfin)
        in_specs.append(pl.BlockSpec(nfin.shape, lambda i: (0, 0)))
    scratch = [pltpu.VMEM((tm, D_MODEL), F32), pltpu.VMEM((tm, D_MODEL), BF16)]
    if not sample:
        scratch.append(pltpu.VMEM((N_FFN_CHUNKS, SUBLANES, c2), F32))
    return pl.pallas_call(
        functools.partial(_ffn_kernel, sample=sample, final=final, tiles_per_seq=tiles_per_seq, t_new=t_new),
        grid=grid,
        in_specs=in_specs,
        out_specs=[row(D_MODEL), cst_spec],
        out_shape=[jax.ShapeDtypeStruct((n, D_MODEL), F32), cst_shape],
        scratch_shapes=scratch,
        compiler_params=_params(1),
    )(*args)


def _block_diag(blocks):
    dpt, n, r, c = blocks.shape
    out = jnp.zeros((dpt, n, r, n, c), blocks.dtype)
    idx = jnp.arange(n)
    out = out.at[:, idx, :, idx, :].set(jnp.moveaxis(blocks, 1, 0))
    return out.reshape(dpt, n * r, n * c)


def _chunk_cols(a):
    lead = a.shape[:-1]
    g = a[..., :FFN_DIM].reshape(lead + (N_FFN_CHUNKS, FFN_CHUNK))
    v = a[..., FFN_DIM:].reshape(lead + (N_FFN_CHUNKS, FFN_CHUNK))
    return jnp.concatenate([g, v], axis=-1)


def _unchunk_cols(a):
    lead = a.shape[:-2]
    g = a[..., :FFN_CHUNK].reshape(lead + (FFN_DIM,))
    v = a[..., FFN_CHUNK:].reshape(lead + (FFN_DIM,))
    return jnp.concatenate([g, v], axis=-1)


def _rope_tables(pos):
    half = MLA_ROPE // 2
    inv_freq = ROPE_THETA ** (-jnp.arange(half, dtype=jnp.float32) / half)
    ang = pos.astype(jnp.float32)[:, None] * inv_freq
    cos, sin = jnp.cos(ang), jnp.sin(ang)
    reps = LANES // MLA_ROPE
    cos_t = jnp.tile(jnp.concatenate([cos, cos], axis=1), (1, reps))
    sin_t = jnp.tile(jnp.concatenate([-sin, sin], axis=1), (1, reps))
    return cos_t, sin_t


def _prep_weights(norm_mix, w_in, mla_q_norm, mla_w_uq, mla_kv_norm, mla_w_uk, mla_w_uv, gmlp_v_norm,
                  w_branch, w_out, norm_ffn, ffn_w_up, ffn_conv_w, ffn_conv_b, ffn_w_down):
    depth = w_in.shape[0]
    o = np.cumsum((0, MLA_Q_LORA, MLA_KV_LORA, MLA_ROPE, GMLP_WIDTH, GMLP_WIDTH, SB_HEADS * SB_HEAD_DIM,
                   SB_KV_HEADS * SB_HEAD_DIM, SB_KV_HEADS * SB_HEAD_DIM, N_BRANCH * D_MODEL)).tolist()
    w1 = jnp.concatenate([w_in[:, :, o[0]:o[2]], jnp.tile(w_in[:, :, o[2]:o[3]], (1, 1, LANES // MLA_ROPE)),
                          w_in[:, :, o[3]:o[8]]], axis=2).astype(BF16)
    wuq3 = mla_w_uq.reshape(depth, MLA_Q_LORA, MLA_HEADS, MLA_NOPE + MLA_ROPE)
    wuq = jnp.concatenate([wuq3[..., :MLA_NOPE].reshape(depth, MLA_Q_LORA, -1),
                           wuq3[..., MLA_NOPE:].reshape(depth, MLA_Q_LORA, -1)], axis=2).astype(BF16)
    cw = jnp.concatenate([_chunk_cols(ffn_conv_w), _chunk_cols(ffn_conv_b)[:, None],
                          jnp.zeros((depth, SUBLANES - CONV_W - 1, N_FFN_CHUNKS, 2 * FFN_CHUNK), F32)], axis=1)
    r3 = lambda a: a.reshape(depth, 1, -1)
    return dict(
        norm_mix=r3(norm_mix), w1=w1, wg=w_in[:, :, o[8]:o[9]].astype(BF16),
        q_norm=r3(mla_q_norm), wuq=wuq, kv_norm=r3(mla_kv_norm), gv_norm=r3(gmlp_v_norm),
        bduk=_block_diag(jnp.transpose(mla_w_uk, (0, 2, 3, 1))).astype(BF16),
        bduv=_block_diag(jnp.transpose(mla_w_uv, (0, 2, 1, 3))).astype(BF16),
        wbr=w_branch.astype(BF16), wout=w_out.astype(BF16), norm_ffn=r3(norm_ffn),
        wup=jnp.moveaxis(_chunk_cols(ffn_w_up), 2, 1).astype(BF16),
        cw=jnp.moveaxis(cw, 2, 1),
        wdn=ffn_w_down.reshape(depth, N_FFN_CHUNKS, FFN_CHUNK, D_MODEL).astype(BF16),
    )


def kernel(x_prompt, x_sample, cache_mla_latent, cache_mla_krope, cache_sb_k, cache_sb_v, state_ffn_conv, page_table, norm_mix, w_in, mla_q_norm, mla_w_uq, mla_kv_norm, mla_w_uk, mla_w_uv, gmlp_v_norm, gmlp_w_s, gmlp_b_s, w_branch, w_out, norm_ffn, ffn_w_up, ffn_conv_w, ffn_conv_b, ffn_w_down, norm_final):
    batch, seq, _ = x_prompt.shape
    nb, t_new, _ = x_sample.shape
    depth = w_in.shape[0]
    n_pool = cache_sb_k.shape[1]
    past_len = page_table.shape[1] * PAGE_SIZE
    assert t_new == SUBLANES and LANES % t_new == 0
    assert seq % ATT_TILE == 0 or seq < ATT_TILE

    wl = _prep_weights(norm_mix, w_in, mla_q_norm, mla_w_uq, mla_kv_norm, mla_w_uk, mla_w_uv, gmlp_v_norm,
                       w_branch, w_out, norm_ffn, ffn_w_up, ffn_conv_w, ffn_conv_b, ffn_w_down)
    nfin = norm_final.reshape(1, D_MODEL)

    n_s = nb * t_new
    tm_s = min(TOKEN_TILE, n_s)
    cos_p, sin_p = _rope_tables(jnp.arange(seq, dtype=jnp.int32))
    cos_s, sin_s = _rope_tables(jnp.tile(past_len + jnp.arange(t_new, dtype=jnp.int32), tm_s // t_new))
    tm_p = min(TOKEN_TILE, batch * seq)
    pos_tiles_p = max(seq // tm_p, 1)

    r = jnp.arange(GMLP_CHUNK)
    mask_p = (r[None, :] <= r[:, None]).astype(F32)
    mask_s = jnp.logical_and(r[None, :] // t_new == r[:, None] // t_new, r[None, :] <= r[:, None]).astype(F32)
    reps = GMLP_CHUNK // t_new
    ws_s = jnp.tile(gmlp_w_s[:, :, :t_new, :t_new], (1, 1, reps, reps))
    bs_p = jnp.broadcast_to(gmlp_b_s[..., None], gmlp_b_s.shape + (GMLP_GROUP_DIM,))
    bs_s = jnp.broadcast_to(jnp.tile(gmlp_b_s[:, :, :t_new], (1, 1, reps))[..., None], bs_p.shape)

    ta = min(ATT_TILE, seq)
    ra = jnp.arange(ta)
    ut_p = (ra[None, :] > ra[:, None]).astype(BF16)
    rk = jnp.arange(SB_KV_HEADS * PAGE_SIZE)
    u_s = (rk[:, None] > rk[None, :]).astype(BF16)

    pt = page_table.reshape(-1)
    cache_krt = jnp.swapaxes(cache_mla_krope, 2, 3)
    cache_k2 = cache_sb_k.reshape(depth, n_pool, PAGE_SIZE * SB_KV_HEADS, SB_HEAD_DIM)
    cache_v2 = cache_sb_v.reshape(depth, n_pool, PAGE_SIZE * SB_KV_HEADS, SB_HEAD_DIM)
    state_c = jnp.moveaxis(_chunk_cols(state_ffn_conv), (2, 3), (1, 2))

    xp = x_prompt.reshape(batch * seq, D_MODEL)
    xs = x_sample.reshape(n_s, D_MODEL)
    outs = [[] for _ in range(11)]
    for l in range(depth):
        last = l == depth - 1
        ql, qr, kc, ckv, kr, ogm, _, sq, skv, sk, sv = _inproj(
            xp, cos_p, sin_p, pos_tiles_p, wl, gmlp_w_s[l], bs_p[l], mask_p, l)
        olat = _mla_prompt(ql, qr, kc, batch, seq)
        osb = _sb_prompt(sq, skv, ut_p, batch, seq)
        x1 = _merge(xp, olat, ogm, osb, wl, l)
        xp, cst = _ffn(x1, wl, l, nfin, None, batch, seq, last)
        outs[0].append(ckv.reshape(batch, seq, MLA_KV_LORA))
        outs[1].append(kr[:, :MLA_ROPE].reshape(batch, seq, MLA_ROPE))
        outs[2].append(sk.reshape(batch, seq, SB_KV_HEADS, SB_HEAD_DIM))
        outs[3].append(sv.reshape(batch, seq, SB_KV_HEADS, SB_HEAD_DIM))
        outs[4].append(_unchunk_cols(jnp.swapaxes(cst[:, :, SUBLANES - 2:, :], 1, 2)))
        ql, qr, kc, ckv, kr, ogm, gv, sq, skv, sk, sv = _inproj(
            xs, cos_s, sin_s, 1, wl, ws_s[l], bs_s[l], mask_s, l)
        r3 = lambda a: a.reshape(nb, t_new, a.shape[-1])
        olat = _mla_sample(pt, r3(ql), r3(qr), r3(kc), cache_mla_latent, cache_krt, l)
        osb = _sb_sample(pt, r3(sq), sk.reshape(nb, t_new * SB_KV_HEADS, SB_HEAD_DIM),
                         sv.reshape(nb, t_new * SB_KV_HEADS, SB_HEAD_DIM), u_s, cache_k2, cache_v2, l)
        x1 = _merge(xs, olat.reshape(n_s, -1), ogm, osb.reshape(n_s, -1), wl, l)
        xs, upf = _ffn(x1, wl, l, nfin, state_c[l], nb, t_new, last)
        outs[5].append(r3(ckv))
        outs[6].append(r3(kr[:, :MLA_ROPE]))
        outs[7].append(sk.reshape(nb, t_new, SB_KV_HEADS, SB_HEAD_DIM))
        outs[8].append(sv.reshape(nb, t_new, SB_KV_HEADS, SB_HEAD_DIM))
        outs[9].append(r3(gv))
        tail = upf.reshape(N_FFN_CHUNKS, nb, t_new, 2 * FFN_CHUNK)[:, :, t_new - (CONV_W - 1):, :]
        outs[10].append(_unchunk_cols(jnp.transpose(tail, (1, 2, 0, 3))))
    return (xp.reshape(batch, seq, D_MODEL), xs.reshape(nb, t_new, D_MODEL)) + tuple(jnp.stack(o) for o in outs)
```
